```python
import math
import jax, jax.numpy as jnp
from jax import lax
import numpy as np

D_MODEL = 1024
BATCH = 16
SEQ = 4096
DEPTH = 4

CTX_LEN = 256
GRID_W = 64
HEAD_DIM = 64
EPS = 1e-6

MLSTM_HEADS = 4
MLSTM_WIDTH = MLSTM_HEADS * HEAD_DIM
MLSTM_CHUNK = 64
GMLP_GROUPS = 4
GMLP_WIDTH = GMLP_GROUPS * HEAD_DIM
GMLP_CHUNK = 128
ATTN_HEADS = 8
ATTN_KV_HEADS = 2
GQA_GROUP = ATTN_HEADS // ATTN_KV_HEADS
ATTN_WIDTH = ATTN_HEADS * HEAD_DIM
KV_WIDTH = ATTN_KV_HEADS * HEAD_DIM
Q_BLOCK = 128
ROPE_THETA = 10000.0
ROPE_PAIRS_PER_AXIS = HEAD_DIM // 4

D_MIX = MLSTM_WIDTH + GMLP_WIDTH + ATTN_WIDTH

COL_MQ = 0
COL_MK = COL_MQ + MLSTM_WIDTH
COL_MV = COL_MK + MLSTM_WIDTH
COL_MO = COL_MV + MLSTM_WIDTH
COL_GATES = COL_MO + MLSTM_WIDTH
N_GATE_COLS = 4 * MLSTM_HEADS
COL_GU = COL_GATES + N_GATE_COLS
COL_GV = COL_GU + GMLP_WIDTH
COL_AQ = COL_GV + GMLP_WIDTH
COL_AK = COL_AQ + ATTN_WIDTH
COL_AV = COL_AK + KV_WIDTH
D_IN = COL_AV + KV_WIDTH
SPLIT_IDX = (COL_MK, COL_MV, COL_MO, COL_GATES, COL_GU, COL_GV, COL_AQ, COL_AK, COL_AV)

N_EXPERTS = 32
TOP_K = 4
D_FF_EXPERT = 1024
SWIGLU_LIMIT = 7.0
SWIGLU_ALPHA = 1.702
MOE_BLOCK = 128

kernel_name = "hymba_style_mlstm_gmlp_gqa_moe_dit"


def rms_norm(x, gain=None):
    xf = x.astype(jnp.float32)
    y = xf * lax.rsqrt(jnp.mean(xf * xf, axis=-1, keepdims=True) + EPS)
    if gain is not None:
        y = y * gain.astype(jnp.float32)
    return y.astype(x.dtype)


def modulate(x, shift, scale):
    return rms_norm(x) * (1 + scale) + shift


def axial_rope(rows):
    row = jnp.repeat(jnp.arange(rows, dtype=jnp.float32), GRID_W)
    col = jnp.tile(jnp.arange(GRID_W, dtype=jnp.float32), rows)
    inv_freq = ROPE_THETA ** (-jnp.arange(ROPE_PAIRS_PER_AXIS, dtype=jnp.float32) / ROPE_PAIRS_PER_AXIS)
    ang = jnp.concatenate([row[:, None] * inv_freq, col[:, None] * inv_freq], axis=-1)
    return jnp.cos(ang), jnp.sin(ang)


def apply_rope(x, cos, sin):
    c = cos[None, :, None, :].astype(x.dtype)
    s = sin[None, :, None, :].astype(x.dtype)
    x1, x2 = x[..., :HEAD_DIM // 2], x[..., HEAD_DIM // 2:]
    return jnp.concatenate([x1 * c - x2 * s, x2 * c + x1 * s], axis=-1)


def mlstm_scan(q, k, v, log_i, log_f, state):
    B, H, T, dh = q.shape
    L = MLSTM_CHUNK
    nc = T // L

    def to_chunks(a):
        return jnp.moveaxis(a.reshape(B, H, nc, L, *a.shape[3:]), 2, 0)

    xs = (to_chunks(q), to_chunks(k), to_chunks(v), to_chunks(log_i), to_chunks(log_f))
    lower = jnp.tril(jnp.ones((L, L), dtype=bool))

    def step(carry, inp):
        C, n, m = carry
        qc, kc, vc, ic, fc = inp
        qf, kf, vf = qc.astype(jnp.float32), kc.astype(jnp.float32), vc.astype(jnp.float32)
        b = jnp.cumsum(fc, axis=-1)
        dmat = jnp.where(lower, b[..., :, None] - b[..., None, :] + ic[..., None, :], -jnp.inf)
        inter = b + m[..., None]
        m_t = jnp.maximum(inter, jnp.max(dmat, axis=-1))
        s = jnp.einsum('bhtd,bhsd->bhts', qf, kf) * jnp.exp(dmat - m_t[..., None])
        w_inter = jnp.exp(inter - m_t)
        num = jnp.einsum('bhts,bhse->bhte', s, vf) + w_inter[..., None] * jnp.einsum('bhtd,bhde->bhte', qf, C)
        den = jnp.sum(s, axis=-1) + w_inter * jnp.einsum('bhtd,bhd->bht', qf, n)
        h = num / jnp.maximum(jnp.abs(den), jnp.exp(-m_t))[..., None]
        b_end = b[..., -1]
        dec = b_end[..., None] - b + ic
        m_new = jnp.maximum(b_end + m, jnp.max(dec, axis=-1))
        ws = jnp.exp(dec - m_new[..., None])
        w_prev = jnp.exp(b_end + m - m_new)
        C_new = w_prev[..., None, None] * C + jnp.einsum('bhs,bhsd,bhse->bhde', ws, kf, vf)
        n_new = w_prev[..., None] * n + jnp.einsum('bhs,bhsd->bhd', ws, kf)
        return (C_new, n_new, m_new), h.astype(q.dtype)

    state, hs = lax.scan(step, state, xs)
    return jnp.moveaxis(hs, 0, 2).reshape(B, H, T, dh), state


def mlstm_prep(q, k, v, g):
    B, T, _ = q.shape

    def heads(a):
        return a.reshape(B, T, MLSTM_HEADS, HEAD_DIM).transpose(0, 2, 1, 3)

    g = g.astype(jnp.float32).reshape(B, T, 4, MLSTM_HEADS).transpose(2, 0, 3, 1)
    return (heads(q), heads(k) * HEAD_DIM ** -0.5, heads(v),
            g[0], jax.nn.log_sigmoid(g[1]), g[2], jax.nn.log_sigmoid(g[3]))


def mlstm_mixer(pc, pl):
    qc, kc, vc, icf, fcf, icb, fcb = mlstm_prep(pc[0], pc[1], pc[2], pc[4])
    ql, kl, vl, ilf, flf, ilb, flb = mlstm_prep(pl[0], pl[1], pl[2], pl[4])
    B = qc.shape[0]
    zero = (jnp.zeros((B, MLSTM_HEADS, HEAD_DIM, HEAD_DIM), jnp.float32),
            jnp.zeros((B, MLSTM_HEADS, HEAD_DIM), jnp.float32),
            jnp.zeros((B, MLSTM_HEADS), jnp.float32))

    def flip(a):
        return jnp.flip(a, axis=2)

    h_cf, s_f = mlstm_scan(qc, kc, vc, icf, fcf, zero)
    h_lf, _ = mlstm_scan(ql, kl, vl, ilf, flf, s_f)
    h_cb, s_b = mlstm_scan(flip(qc), flip(kc), flip(vc), flip(icb), flip(fcb), zero)
    h_lb, _ = mlstm_scan(flip(ql), flip(kl), flip(vl), flip(ilb), flip(flb), s_b)

    def out(h_f, h_b, o):
        h = h_f + flip(h_b)
        Bh, H, T, dh = h.shape
        return jax.nn.sigmoid(o) * h.transpose(0, 2, 1, 3).reshape(Bh, T, H * dh)

    return out(h_cf, h_cb, pc[3]), out(h_lf, h_lb, pl[3])


def chunk_mlp(u, v, gmlp_norm, ws, bias):
    B, T, _ = u.shape
    u = jax.nn.gelu(u)
    v = rms_norm(jax.nn.gelu(v), gmlp_norm)
    vc = v.reshape(B, T // GMLP_CHUNK, GMLP_CHUNK, GMLP_GROUPS, HEAD_DIM)
    mixed = jnp.einsum('gpr,bnrgc->bnpgc', ws, vc) + bias.T[:, :, None]
    return u * mixed.reshape(B, T, GMLP_WIDTH)


def attn_qkv(q, k, v, q_norm, k_norm, cos, sin):
    B, T, _ = q.shape
    q = rms_norm(q.reshape(B, T, ATTN_HEADS, HEAD_DIM), q_norm)
    k = rms_norm(k.reshape(B, T, ATTN_KV_HEADS, HEAD_DIM), k_norm)
    if cos is not None:
        q = apply_rope(q, cos, sin)
        k = apply_rope(k, cos, sin)
    return (q.reshape(B, T, ATTN_KV_HEADS, GQA_GROUP, HEAD_DIM), k,
            v.reshape(B, T, ATTN_KV_HEADS, HEAD_DIM))


def gqa_block(q, k, v):
    s = jnp.einsum('bqkgd,bskd->bkgqs', q, k).astype(jnp.float32) * HEAD_DIM ** -0.5
    p = jax.nn.softmax(s, axis=-1).astype(v.dtype)
    o = jnp.einsum('bkgqs,bskd->bqkgd', p, v)
    return o.reshape(o.shape[0], o.shape[1], ATTN_WIDTH)


def latent_attention(q, k_all, v_all):
    B, T = q.shape[:2]
    nb = T // Q_BLOCK
    qb = jnp.swapaxes(q.reshape(B, nb, Q_BLOCK, ATTN_KV_HEADS, GQA_GROUP, HEAD_DIM), 0, 1)
    out = lax.map(lambda qq: gqa_block(qq, k_all, v_all), qb)
    return jnp.swapaxes(out, 0, 1).reshape(B, T, ATTN_WIDTH)


def merge_heads(y, mix_norm):
    B, T, _ = y.shape
    y = rms_norm(y.reshape(B, T, D_MIX // HEAD_DIM, HEAD_DIM)).reshape(B, T, D_MIX)
    return y * mix_norm


def token_mixers(hc, hl, cos, sin, w_in, b_in, q_norm, k_norm, gmlp_norm, gmlp_ws, gmlp_b, mix_norm, ctx_out):
    pc = jnp.split(hc @ w_in + b_in, SPLIT_IDX, axis=-1)
    pl = jnp.split(hl @ w_in + b_in, SPLIT_IDX, axis=-1)
    m_c, m_l = mlstm_mixer(pc, pl)
    g_l = chunk_mlp(pl[5], pl[6], gmlp_norm, gmlp_ws, gmlp_b)
    qc, kc, vc = attn_qkv(pc[7], pc[8], pc[9], q_norm, k_norm, None, None)
    ql, kl, vl = attn_qkv(pl[7], pl[8], pl[9], q_norm, k_norm, cos, sin)
    a_l = latent_attention(ql, jnp.concatenate([kc, kl], axis=1), jnp.concatenate([vc, vl], axis=1))
    y_l = merge_heads(jnp.concatenate([m_l, g_l, a_l], axis=-1), mix_norm)
    if not ctx_out:
        return None, y_l
    g_c = chunk_mlp(pc[5], pc[6], gmlp_norm, gmlp_ws, gmlp_b)
    a_c = gqa_block(qc, kc, vc)
    y_c = merge_heads(jnp.concatenate([m_c, g_c, a_c], axis=-1), mix_norm)
    return y_c, y_l


def moe_ffn(h, router_w, router_b, w_gate_up, b_gate_up, w_down, b_down):
    n_tok, d = h.shape
    logits = (h @ router_w + router_b).astype(jnp.float32)
    top_val, top_idx = lax.top_k(logits, TOP_K)
    gates = jax.nn.softmax(top_val, axis=-1).astype(h.dtype)
    flat_e = top_idx.reshape(-1)
    n_assign = n_tok * TOP_K
    order = jnp.argsort(flat_e, stable=True)
    sorted_e = flat_e[order]
    sorted_tok = (order // TOP_K).astype(jnp.int32)
    sorted_gate = gates.reshape(-1)[order]
    counts = jnp.bincount(flat_e, length=N_EXPERTS)
    padded = (counts + MOE_BLOCK - 1) // MOE_BLOCK * MOE_BLOCK
    ends_padded = jnp.cumsum(padded)
    start_padded = ends_padded - padded
    start = jnp.cumsum(counts) - counts
    dest = start_padded[sorted_e] + jnp.arange(n_assign) - start[sorted_e]
    n_blocks = -(-n_assign // MOE_BLOCK) + N_EXPERTS
    n_slots = n_blocks * MOE_BLOCK
    slot_tok = jnp.full((n_slots,), n_tok, jnp.int32).at[dest].set(sorted_tok)
    slot_gate = jnp.zeros((n_slots,), h.dtype).at[dest].set(sorted_gate)
    block_expert = jnp.minimum(
        jnp.searchsorted(ends_padded, jnp.arange(n_blocks) * MOE_BLOCK, side='right'), N_EXPERTS - 1)
    h_pad = jnp.concatenate([h, jnp.zeros((1, d), h.dtype)], axis=0)

    def expert_block(args):
        tok_idx, e = args
        gu = h_pad[tok_idx] @ w_gate_up[e] + b_gate_up[e]
        x_glu = jnp.minimum(gu[:, 0::2], SWIGLU_LIMIT)
        x_lin = jnp.clip(gu[:, 1::2], -SWIGLU_LIMIT, SWIGLU_LIMIT)
        act = x_glu * jax.nn.sigmoid(SWIGLU_ALPHA * x_glu) * (x_lin + 1)
        return act @ w_down[e] + b_down[e]

    y_slots = lax.map(expert_block, (slot_tok.reshape(n_blocks, MOE_BLOCK), block_expert))
    y = jnp.zeros((n_tok + 1, d), h.dtype).at[slot_tok].add(y_slots.reshape(n_slots, d) * slot_gate[:, None])
    return y[:n_tok]


def setup_inputs(seed: int = 0) -> dict:
    key = jax.random.key(seed)
    ks = jax.random.split(key, 22)
    f32 = jnp.float32
    nrm = lambda k, shape, s: jax.random.normal(k, shape, f32) * s
    b_in = nrm(ks[7], (DEPTH, D_IN), 0.02)
    forget_bias = jnp.linspace(3.0, 6.0, MLSTM_HEADS, dtype=f32)
    b_in = b_in.at[:, COL_GATES + MLSTM_HEADS:COL_GATES + 2 * MLSTM_HEADS].add(forget_bias)
    b_in = b_in.at[:, COL_GATES + 3 * MLSTM_HEADS:COL_GATES + 4 * MLSTM_HEADS].add(forget_bias)
    return {
        "x": nrm(ks[0], (BATCH, SEQ, D_MODEL), 1.0),
        "c": nrm(ks[1], (BATCH, D_MODEL), 1.0),
        "ctx": nrm(ks[2], (BATCH, CTX_LEN, D_MODEL), 1.0),
        "c_ctx": nrm(ks[3], (D_MODEL,), 1.0),
        "w_ada": nrm(ks[4], (DEPTH, D_MODEL, 6 * D_MODEL), 0.2 * D_MODEL ** -0.5),
        "b_ada": nrm(ks[5], (DEPTH, 6 * D_MODEL), 0.02),
        "w_in": nrm(ks[6], (DEPTH, D_MODEL, D_IN), D_MODEL ** -0.5),
        "b_in": b_in,
        "q_norm": 1.0 + nrm(ks[8], (DEPTH, HEAD_DIM), 0.02),
        "k_norm": 1.0 + nrm(ks[9], (DEPTH, HEAD_DIM), 0.02),
        "gmlp_norm": 1.0 + nrm(ks[10], (DEPTH, GMLP_WIDTH), 0.02),
        "gmlp_ws": nrm(ks[11], (DEPTH, GMLP_GROUPS, GMLP_CHUNK, GMLP_CHUNK), GMLP_CHUNK ** -0.5),
        "gmlp_b": 1.0 + nrm(ks[12], (DEPTH, GMLP_GROUPS, GMLP_CHUNK), 0.1),
        "mix_norm": 1.0 + nrm(ks[13], (DEPTH, D_MIX), 0.02),
        "w_out": nrm(ks[14], (DEPTH, D_MIX, D_MODEL), D_MIX ** -0.5),
        "router_w": nrm(ks[15], (DEPTH, D_MODEL, N_EXPERTS), D_MODEL ** -0.5),
        "router_b": nrm(ks[16], (DEPTH, N_EXPERTS), 0.01),
        "w_gate_up": nrm(ks[17], (DEPTH, N_EXPERTS, D_MODEL, 2 * D_FF_EXPERT), D_MODEL ** -0.5),
        "b_gate_up": nrm(ks[18], (DEPTH, N_EXPERTS, 2 * D_FF_EXPERT), 0.02),
        "w_down": nrm(ks[19], (DEPTH, N_EXPERTS, D_FF_EXPERT, D_MODEL), D_FF_EXPERT ** -0.5),
        "b_down": nrm(ks[20], (DEPTH, N_EXPERTS, D_MODEL), 0.02),
        "final_norm": 1.0 + nrm(ks[21], (D_MODEL,), 0.02),
    }


def reference(x, c, ctx, c_ctx, w_ada, b_ada, w_in, b_in, q_norm, k_norm, gmlp_norm, gmlp_ws, gmlp_b,
              mix_norm, w_out, router_w, router_b, w_gate_up, b_gate_up, w_down, b_down, final_norm):
    B, T, D = x.shape
    rows = T // GRID_W
    cos, sin = axial_rope(rows)
    xl, xc = x, ctx
    Bc, Tc = xc.shape[0], xc.shape[1]
    for l in range(DEPTH):
        ctx_out = l < DEPTH - 1
        mod_l = jax.nn.silu(c) @ w_ada[l] + b_ada[l]
        mod_c = jax.nn.silu(c_ctx) @ w_ada[l] + b_ada[l]
        sh1, sc1, g1, sh2, sc2, g2 = jnp.split(mod_l[:, None, :], 6, axis=-1)
        csh1, csc1, cg1, csh2, csc2, cg2 = jnp.split(mod_c[None, None, :], 6, axis=-1)
        hl = modulate(xl, sh1, sc1)
        hc = modulate(xc, csh1, csc1)
        y_c, y_l = token_mixers(hc, hl, cos, sin, w_in[l], b_in[l], q_norm[l], k_norm[l], gmlp_norm[l],
                                gmlp_ws[l], gmlp_b[l], mix_norm[l], ctx_out)
        xl = xl + g1 * (y_l @ w_out[l])
        hl2 = modulate(xl, sh2, sc2)
        if ctx_out:
            xc = xc + cg1 * (y_c @ w_out[l])
            hc2 = modulate(xc, csh2, csc2)
            tokens = jnp.concatenate([hc2.reshape(Bc * Tc, D), hl2.reshape(B * T, D)], axis=0)
            f = moe_ffn(tokens, router_w[l], router_b[l], w_gate_up[l], b_gate_up[l], w_down[l], b_down[l])
            xc = xc + cg2 * f[:Bc * Tc].reshape(Bc, Tc, D)
            xl = xl + g2 * f[Bc * Tc:].reshape(B, T, D)
        else:
            f = moe_ffn(hl2.reshape(B * T, D), router_w[l], router_b[l], w_gate_up[l], b_gate_up[l],
                        w_down[l], b_down[l])
            xl = xl + g2 * f.reshape(B, T, D)
    return rms_norm(xl, final_norm)
```

```python
import functools

import jax
import jax.numpy as jnp
from jax import lax
from jax.experimental import pallas as pl
from jax.experimental.pallas import tpu as pltpu

F32 = jnp.float32
BF16 = jnp.bfloat16

HEAD_DIM = 64
EPS = 1e-6
GRID_W = 64
ROPE_THETA = 10000.0
MLSTM_HEADS = 4
MLSTM_CHUNK = 64
GMLP_GROUPS = 4
GMLP_CHUNK = 128
ATTN_HEADS = 8
ATTN_KV_HEADS = 2
N_EXPERTS = 32
TOP_K = 4
SWIGLU_LIMIT = 7.0
SWIGLU_ALPHA = 1.702
LANES = 128
NEG_BIG = -1e30

C_MQ, C_MK, C_MV, C_MO, C_GU, C_GV, C_AQ, C_AK, C_AV, C_G, C_END = (
    0, 256, 512, 768, 1024, 1280, 1536, 2048, 2176, 2304, 2432)


def _dot(a, b):
    return jnp.dot(a, b, preferred_element_type=F32)


def _dot_nt(a, b):
    return lax.dot_general(a, b, (((1,), (1,)), ((), ())), preferred_element_type=F32)


def _split(x):
    hi = x.astype(BF16)
    lo = (x - hi.astype(F32)).astype(BF16)
    return hi, lo


def _cparams(ndims, vmem_mb):
    return pltpu.CompilerParams(dimension_semantics=("arbitrary",) * ndims,
                                vmem_limit_bytes=vmem_mb << 20)


def _sigmoid(x):
    return 1.0 / (1.0 + jnp.exp(-x))


def _gelu_tanh(x):
    return 0.5 * x * (1.0 + jnp.tanh(0.7978845608028654 * (x + 0.044715 * (x * x * x))))


def _rms(x):
    return x * lax.rsqrt(jnp.mean(x * x, axis=-1, keepdims=True) + EPS)


def _seg_indicator(n):
    r = lax.broadcasted_iota(jnp.int32, (n, n), 0) // HEAD_DIM
    c = lax.broadcasted_iota(jnp.int32, (n, n), 1) // HEAD_DIM
    return jnp.where(r == c, 1.0, 0.0).astype(BF16)


def _head_rms(x, ind):
    hi, lo = _split(x * x)
    ss = _dot(hi, ind) + _dot(lo, ind)
    return x * lax.rsqrt(ss * (1.0 / HEAD_DIM) + EPS)


def _ada_kernel(c_ref, w_ref, b_ref, o_ref):
    c = c_ref[...]
    s = c * _sigmoid(c)
    s_hi, s_lo = _split(s)
    w_hi, w_lo = _split(w_ref[0])
    o_ref[0] = _dot(s_hi, w_hi) + _dot(s_lo, w_hi) + _dot(s_hi, w_lo) + b_ref[0]


def _ada_call(cs, w_ada, b_ada):
    depth, d, d6 = w_ada.shape
    r = cs.shape[0]
    tn = 1536
    return pl.pallas_call(
        _ada_kernel,
        grid=(depth, d6 // tn),
        in_specs=[pl.BlockSpec((r, d), lambda l, j: (0, 0)),
                  pl.BlockSpec((1, d, tn), lambda l, j: (l, 0, j)),
                  pl.BlockSpec((1, 1, tn), lambda l, j: (l, 0, j))],
        out_specs=pl.BlockSpec((1, r, tn), lambda l, j: (l, 0, j)),
        out_shape=jax.ShapeDtypeStruct((depth, r, d6), F32),
        compiler_params=_cparams(2, 40),
        name="ada",
    )(cs, w_ada, b_ada.reshape(depth, 1, d6))


def _proj_kernel(x_ref, mod_ref, w_ref, b_ref, qn_ref, kn_ref, gn_ref, ws_ref, gb_ref, cos_ref, sin_ref,
                 mq_ref, mk_ref, mv_ref, mo_ref, g_ref, gm_ref, aq_ref, ak_ref, av_ref, *, tm):
    h = (_rms(x_ref[...]) * (1.0 + mod_ref[0, 1:2, :]) + mod_ref[0, 0:1, :]).astype(BF16)

    def seg(a, b):
        return _dot(h, w_ref[:, a:b]) + b_ref[:, a:b]

    mq_ref[...] = seg(C_MQ, C_MK).astype(BF16)
    mk_ref[...] = (seg(C_MK, C_MV) * HEAD_DIM ** -0.5).astype(BF16)
    mv_ref[...] = seg(C_MV, C_MO).astype(BF16)
    mo_ref[...] = seg(C_MO, C_GU)

    g = seg(C_G, C_END)
    ls = jnp.minimum(g, 0.0) - jnp.log1p(jnp.exp(-jnp.abs(g)))
    r = lax.broadcasted_iota(jnp.int32, (tm, tm), 0)
    c = lax.broadcasted_iota(jnp.int32, (tm, tm), 1)
    same = (r // MLSTM_CHUNK) == (c // MLSTM_CHUNK)
    pre = jnp.where(same, jnp.where(c <= r, 1.0, 0.0), 0.0).astype(BF16)
    suf = jnp.where(same, jnp.where(c >= r, 1.0, 0.0), 0.0).astype(BF16)
    ls_hi, ls_lo = _split(ls)
    psum = _dot(pre, ls_hi) + _dot(pre, ls_lo)
    ssum = _dot(suf, ls_hi) + _dot(suf, ls_lo)
    lane = lax.broadcasted_iota(jnp.int32, (tm, LANES), 1)
    q4 = lane // MLSTM_HEADS
    g_ref[...] = jnp.where(q4 == 1, psum, jnp.where(q4 == 3, ssum, g))

    u = _gelu_tanh(seg(C_GU, C_GV))
    v = (_rms(_gelu_tanh(seg(C_GV, C_AQ))) * gn_ref[...]).astype(BF16)
    grp = lax.broadcasted_iota(jnp.int32, (GMLP_CHUNK, 2 * LANES), 1) // HEAD_DIM
    for ci in range(tm // GMLP_CHUNK):
        rows = slice(ci * GMLP_CHUNK, (ci + 1) * GMLP_CHUNK)
        full = _dot(ws_ref[...], v[rows, :])
        mixed = gb_ref[...]
        for gi in range(GMLP_GROUPS):
            mixed = mixed + jnp.where(grp == gi, full[gi * GMLP_CHUNK:(gi + 1) * GMLP_CHUNK, :], 0.0)
        gm_ref[rows, :] = u[rows, :] * mixed

    cos = cos_ref[...]
    sin = sin_ref[...]
    first_half = (lane % HEAD_DIM) < (HEAD_DIM // 2)
    lo_half = lane < HEAD_DIM

    def rope(blk):
        partner = jnp.where(first_half, pltpu.roll(blk, LANES - HEAD_DIM // 2, 1),
                            pltpu.roll(blk, HEAD_DIM // 2, 1))
        return blk * cos + partner * sin

    ind256 = _seg_indicator(2 * LANES)
    for s in range(2):
        qn = _head_rms(seg(C_AQ + s * 256, C_AQ + (s + 1) * 256), ind256) * qn_ref[:, s * 256:(s + 1) * 256]
        for j in range(2):
            blk = rope(qn[:, j * LANES:(j + 1) * LANES]) * HEAD_DIM ** -0.5
            aq_ref[:, s * 256 + j * LANES: s * 256 + (j + 1) * LANES] = blk.astype(BF16)

    kr = rope(_head_rms(seg(C_AK, C_AV), ind256[:LANES, :LANES]) * kn_ref[...])
    kroll = pltpu.roll(kr, HEAD_DIM, 1)
    ak_ref[:, 0:LANES] = jnp.where(lo_half, kr, kroll).astype(BF16)
    ak_ref[:, LANES:2 * LANES] = jnp.where(lo_half, kroll, kr).astype(BF16)

    vf = seg(C_AV, C_G)
    vroll = pltpu.roll(vf, HEAD_DIM, 1)
    av_ref[:, 0:LANES] = jnp.where(lo_half, vf, 1.0).astype(BF16)
    av_ref[:, LANES:2 * LANES] = jnp.where(lo_half, 1.0, vroll).astype(BF16)
    av_ref[:, 2 * LANES:3 * LANES] = jnp.where(lo_half, vroll, 1.0).astype(BF16)
    av_ref[:, 3 * LANES:4 * LANES] = jnp.where(lo_half, 1.0, vf).astype(BF16)


def _proj_call(x, mods, w, b, qn, kn, gn, ws4, gb, cos_t, sin_t, *, l, tm, nct, tpb):
    n, d = x.shape

    def mod_idx(i):
        return (l, jnp.where(i < nct, 0, 1 + (i - nct) // tpb), 0, 0)

    def rope_idx(i):
        return (jnp.where(i < nct, 0, 1 + (i - nct) % tpb), 0)

    row = lambda i: (i, 0)
    const2 = lambda i: (0, 0)
    lay = lambda i: (l, 0, 0)
    outs = [(256, BF16), (256, BF16), (256, BF16), (256, F32), (LANES, F32), (256, F32),
            (512, BF16), (256, BF16), (512, BF16)]
    return pl.pallas_call(
        functools.partial(_proj_kernel, tm=tm),
        grid=(n // tm,),
        in_specs=[pl.BlockSpec((tm, d), row),
                  pl.BlockSpec((None, 1, 6, d), mod_idx),
                  pl.BlockSpec((None, d, C_END), lay),
                  pl.BlockSpec((None, 1, C_END), lay),
                  pl.BlockSpec((None, 1, 512), lay),
                  pl.BlockSpec((None, 1, LANES), lay),
                  pl.BlockSpec((None, 1, 256), lay),
                  pl.BlockSpec((None, GMLP_GROUPS * GMLP_CHUNK, GMLP_CHUNK), lay),
                  pl.BlockSpec((None, GMLP_CHUNK, 256), lay),
                  pl.BlockSpec((tm, LANES), rope_idx),
                  pl.BlockSpec((tm, LANES), rope_idx)],
        out_specs=[pl.BlockSpec((tm, wd), row) for wd, _ in outs],
        out_shape=[jax.ShapeDtypeStruct((n, wd), dt) for wd, dt in outs],
        compiler_params=_cparams(1, 48),
        name="proj",
    )(x, mods, w, b, qn, kn, gn, ws4, gb, cos_t, sin_t)


def _mlstm_kernel(qf_ref, kf_ref, vf_ref, gf_ref, gtf_ref, qb_ref, kb_ref, vb_ref, gb_ref, gtb_ref,
                  hf_ref, hb_ref, c_ref, m_ref, *, nch):
    ch = MLSTM_CHUNK

    @pl.when(pl.program_id(1) == 0)
    def _():
        c_ref[...] = jnp.zeros_like(c_ref)
        m_ref[...] = jnp.zeros_like(m_ref)

    lane = lax.broadcasted_iota(jnp.int32, (ch, LANES), 1)
    lo_half = lane < HEAD_DIM
    rr = lax.broadcasted_iota(jnp.int32, (ch, ch), 0)
    cc = lax.broadcasted_iota(jnp.int32, (ch, ch), 1)
    dirs = ((qf_ref, kf_ref, vf_ref, gf_ref, gtf_ref, hf_ref), (qb_ref, kb_ref, vb_ref, gb_ref, gtb_ref, hb_ref))
    for d, (q_ref, k_ref, v_ref, g_ref, gt_ref, h_ref) in enumerate(dirs):
        mask = (cc <= rr) if d == 0 else (cc >= rr)
        order = range(nch) if d == 0 else range(nch - 1, -1, -1)
        for ci in order:
            rows = slice(ci * ch, (ci + 1) * ch)
            tl = slice(ci * LANES, ci * LANES + ch)
            for p in range(MLSTM_HEADS // 2):
                cols = slice(p * LANES, (p + 1) * LANES)
                qp = q_ref[rows, cols]
                kp = k_ref[rows, cols]
                vp = v_ref[rows, cols]
                res = []
                for e in range(2):
                    hd = 2 * p + e
                    st = d * MLSTM_HEADS + hd
                    sel = lo_half if e == 0 else jnp.logical_not(lo_half)
                    km = jnp.where(sel, kp, jnp.zeros_like(kp))
                    va = jnp.where(sel, vp, jnp.ones_like(vp))
                    gi = d * 2 * MLSTM_HEADS + hd
                    gb_ = gi + MLSTM_HEADS
                    ic = g_ref[rows, gi:gi + 1]
                    bc = g_ref[rows, gb_:gb_ + 1]
                    ir = gt_ref[gi:gi + 1, tl]
                    br = gt_ref[gb_:gb_ + 1, tl]
                    m = m_ref[st][0:1, 0:1]
                    cst = c_ref[st]
                    dm = jnp.where(mask, bc - br + ir, -jnp.inf)
                    inter = bc + m
                    mt = jnp.maximum(inter, jnp.max(dm, axis=-1, keepdims=True))
                    sc = _dot_nt(qp, km) * jnp.exp(dm - mt)
                    wi = jnp.exp(inter - mt)
                    res.append((_dot(sc.astype(BF16), va) + wi * _dot(qp, cst.astype(BF16)), jnp.exp(-mt)))
                    bend = bc[ch - 1:ch, :] if d == 0 else bc[0:1, :]
                    mnew = jnp.maximum(bend + m, jnp.max(bend - br + ir, axis=-1, keepdims=True))
                    wsc = jnp.exp(bend - bc + ic - mnew)
                    kw = (km.astype(F32) * wsc).T.astype(BF16)
                    c_ref[st] = jnp.exp(bend + m - mnew) * cst + _dot(kw, va)
                    m_ref[st] = jnp.broadcast_to(mnew, (8, LANES))
                (ra, fa), (rb, fb) = res
                num = jnp.where(lo_half, ra, rb)
                den = pltpu.roll(jnp.where(lo_half, rb, ra), HEAD_DIM, 1)
                floor = jnp.where(lo_half, fa, fb)
                h_ref[rows, cols] = num / jnp.maximum(jnp.abs(den), floor)


def _mlstm_call(mq, mk, mv, g, gt2, *, b, rs, nsc, nsl, ncb):
    n = mq.shape[0]
    nch = rs // MLSTM_CHUNK

    def fwd(bi, j):
        return jnp.where(j < nsc, bi * nsc + j, ncb + bi * nsl + (j - nsc))

    def bwd(bi, j):
        return jnp.where(j < nsc, bi * nsc + (nsc - 1 - j), ncb + bi * nsl + (nsl - 1 - (j - nsc)))

    def specs(idx):
        return [pl.BlockSpec((rs, 256), lambda bi, j: (idx(bi, j), 0)),
                pl.BlockSpec((rs, 256), lambda bi, j: (idx(bi, j), 0)),
                pl.BlockSpec((rs, 256), lambda bi, j: (idx(bi, j), 0)),
                pl.BlockSpec((rs, LANES), lambda bi, j: (idx(bi, j), 0)),
                pl.BlockSpec((16, nch * LANES), lambda bi, j: (0, idx(bi, j)))]

    return pl.pallas_call(
        functools.partial(_mlstm_kernel, nch=nch),
        grid=(b, nsc + nsl),
        in_specs=specs(fwd) + specs(bwd),
        out_specs=[pl.BlockSpec((rs, 256), lambda bi, j: (fwd(bi, j), 0)),
                   pl.BlockSpec((rs, 256), lambda bi, j: (bwd(bi, j), 0))],
        out_shape=[jax.ShapeDtypeStruct((n, 256), F32)] * 2,
        scratch_shapes=[pltpu.VMEM((2 * MLSTM_HEADS, LANES, LANES), F32),
                        pltpu.VMEM((2 * MLSTM_HEADS, 8, LANES), F32)],
        compiler_params=_cparams(2, 32),
        name="mlstm",
    )(mq, mk, mv, g, gt2, mq, mk, mv, g, gt2)


def _attn_kernel(q_ref, k_ref, v_ref, o_ref, qs_ref, m_ref, aa_ref, ab_ref, *, tq, tc, t, tk, nqc):
    lane = lax.broadcasted_iota(jnp.int32, (tq, LANES), 1)
    lo_half = lane < HEAD_DIM
    for g in range(ATTN_KV_HEADS):
        for pp in range(2):
            qp = q_ref[:, (2 * g + pp) * LANES:(2 * g + pp + 1) * LANES]
            zero = jnp.zeros_like(qp)
            qs_ref[g, pp * tq:(pp + 1) * tq, :] = jnp.where(lo_half, qp, zero)
            qs_ref[g, (2 + pp) * tq:(3 + pp) * tq, :] = jnp.where(lo_half, zero, qp)
    m_ref[...] = jnp.full_like(m_ref, NEG_BIG)
    aa_ref[...] = jnp.zeros_like(aa_ref)
    ab_ref[...] = jnp.zeros_like(ab_ref)

    def chunk(start, size):
        for g in range(ATTN_KV_HEADS):
            kc = k_ref[0, pl.ds(start, size), g * LANES:(g + 1) * LANES]
            s = _dot_nt(qs_ref[g], kc)
            m_old = m_ref[g]
            m_new = jnp.maximum(m_old, jnp.max(s, axis=-1, keepdims=True))
            alpha = jnp.exp(m_old - m_new)
            p = jnp.exp(s - jnp.concatenate([m_new] * (size // LANES), axis=1)).astype(BF16)
            va = v_ref[0, pl.ds(start, size), (2 * g) * LANES:(2 * g + 1) * LANES]
            vb = v_ref[0, pl.ds(start, size), (2 * g + 1) * LANES:(2 * g + 2) * LANES]
            aa_ref[g] = alpha[:2 * tq] * aa_ref[g] + _dot(p[:2 * tq], va)
            ab_ref[g] = alpha[2 * tq:] * ab_ref[g] + _dot(p[2 * tq:], vb)
            m_ref[g] = m_new

    for c0 in range(0, tc, tk):
        chunk(c0, min(tk, tc - c0))

    @pl.when(pl.program_id(1) >= nqc)
    def _():
        def body(i, carry):
            chunk(pl.multiple_of(tc + i * tk, LANES), tk)
            return carry
        lax.fori_loop(0, t // tk, body, 0)

    for g in range(ATTN_KV_HEADS):
        for pp in range(2):
            a = aa_ref[g, pp * tq:(pp + 1) * tq, :]
            bq = ab_ref[g, pp * tq:(pp + 1) * tq, :]
            oa = a * pltpu.roll(1.0 / a, HEAD_DIM, 1)
            ob = bq * pltpu.roll(1.0 / bq, HEAD_DIM, 1)
            o_ref[:, (2 * g + pp) * LANES:(2 * g + pp + 1) * LANES] = jnp.where(lo_half, oa, ob)


def _attn_call(aq, kd, vd, *, b, tq, tc, t, tk, ncq):
    n = aq.shape[0]
    nqc, nql = tc // tq, t // tq

    def qidx(bi, j):
        return (jnp.where(j < nqc, bi * nqc + j, ncq + bi * nql + (j - nqc)), 0)

    return pl.pallas_call(
        functools.partial(_attn_kernel, tq=tq, tc=tc, t=t, tk=tk, nqc=nqc),
        grid=(b, nqc + nql),
        in_specs=[pl.BlockSpec((tq, 512), qidx),
                  pl.BlockSpec((1, tc + t, 256), lambda bi, j: (bi, 0, 0)),
                  pl.BlockSpec((1, tc + t, 512), lambda bi, j: (bi, 0, 0))],
        out_specs=pl.BlockSpec((tq, 512), qidx),
        out_shape=jax.ShapeDtypeStruct((n, 512), F32),
        scratch_shapes=[pltpu.VMEM((ATTN_KV_HEADS, 4 * tq, LANES), BF16),
                        pltpu.VMEM((ATTN_KV_HEADS, 4 * tq, LANES), F32),
                        pltpu.VMEM((ATTN_KV_HEADS, 2 * tq, LANES), F32),
                        pltpu.VMEM((ATTN_KV_HEADS, 2 * tq, LANES), F32)],
        compiler_params=_cparams(2, 48),
        name="attn",
    )(aq, kd, vd)


def _merge_kernel(hf_ref, hb_ref, mo_ref, gm_ref, a_ref, x_ref, mod_ref, mn_ref, wo_ref, rwh_ref, rwl_ref, rb_ref,
                  xo_ref, h2_ref, lg_ref):
    ind = _seg_indicator(2 * LANES)
    parts = [_sigmoid(mo_ref[...]) * (hf_ref[...] + hb_ref[...]), gm_ref[...],
             a_ref[:, 0:256], a_ref[:, 256:512]]
    y = jnp.concatenate([_head_rms(pt, ind) for pt in parts], axis=1) * mn_ref[...]
    xn = x_ref[...] + mod_ref[0, 2:3, :] * _dot(y.astype(BF16), wo_ref[...])
    xo_ref[...] = xn
    h2 = _rms(xn) * (1.0 + mod_ref[0, 4:5, :]) + mod_ref[0, 3:4, :]
    h2_ref[...] = h2
    hi, lo = _split(h2)
    lg_ref[...] = _dot(hi, rwh_ref[...]) + _dot(lo, rwh_ref[...]) + _dot(hi, rwl_ref[...]) + rb_ref[...]


def _merge_call(hf, hb, mo, gm, a, x, mods, mn, wo, rwh, rwl, rb, *, l, tm, nct, tpb):
    n, d = x.shape

    def mod_idx(i):
        return (l, jnp.where(i < nct, 0, 1 + (i - nct) // tpb), 0, 0)

    row = lambda i: (i, 0)
    lay = lambda i: (l, 0, 0)
    return pl.pallas_call(
        _merge_kernel,
        grid=(n // tm,),
        in_specs=[pl.BlockSpec((tm, 256), row), pl.BlockSpec((tm, 256), row), pl.BlockSpec((tm, 256), row),
                  pl.BlockSpec((tm, 256), row), pl.BlockSpec((tm, 512), row), pl.BlockSpec((tm, d), row),
                  pl.BlockSpec((None, 1, 6, d), mod_idx),
                  pl.BlockSpec((None, 1, d), lay),
                  pl.BlockSpec((None, d, d), lay),
                  pl.BlockSpec((None, d, LANES), lay),
                  pl.BlockSpec((None, d, LANES), lay),
                  pl.BlockSpec((None, 1, LANES), lay)],
        out_specs=[pl.BlockSpec((tm, d), row), pl.BlockSpec((tm, d), row), pl.BlockSpec((tm, LANES), row)],
        out_shape=[jax.ShapeDtypeStruct((n, d), F32), jax.ShapeDtypeStruct((n, d), F32),
                   jax.ShapeDtypeStruct((n, LANES), F32)],
        compiler_params=_cparams(1, 48),
        name="merge",
    )(hf, hb, mo, gm, a, x, mods, mn, wo, rwh, rwl, rb)


def _moe_kernel(be_ref, nv_ref, tok_ref, h_hbm, wg_ref, wl_ref, wd_ref, bg_ref, bl_ref, bd_ref, y_ref,
                xbuf, sem, *, tmb):
    i = pl.program_id(0)

    @pl.when(i < nv_ref[0])
    def _():
        def issue(r, carry):
            tok = tok_ref[0, 0, r]
            pltpu.make_async_copy(h_hbm.at[pl.ds(tok, 1)], xbuf.at[pl.ds(r, 1)], sem).start()
            return carry
        lax.fori_loop(0, tmb, issue, 0, unroll=8)
        def drain(r, carry):
            tok = tok_ref[0, 0, r]
            pltpu.make_async_copy(h_hbm.at[pl.ds(tok, 1)], xbuf.at[pl.ds(r, 1)], sem).wait()
            return carry
        lax.fori_loop(0, tmb, drain, 0, unroll=8)
        xb = xbuf[...].astype(BF16)
        x_glu = jnp.minimum(_dot(xb, wg_ref[...]) + bg_ref[...], SWIGLU_LIMIT)
        x_lin = jnp.clip(_dot(xb, wl_ref[...]) + bl_ref[...], -SWIGLU_LIMIT, SWIGLU_LIMIT)
        act = x_glu * _sigmoid(SWIGLU_ALPHA * x_glu) * (x_lin + 1.0)
        y_ref[...] = _dot(act.astype(BF16), wd_ref[...]) + bd_ref[...]

    @pl.when(i >= nv_ref[0])
    def _():
        y_ref[...] = jnp.zeros_like(y_ref)


def _moe_call(block_expert, nvalid, slot_tok, h2, wg, wl, wd, bg, bl, bd, *, l, tmb):
    n, d = h2.shape
    nb = block_expert.shape[0]
    f = wg.shape[-1]
    wmap = lambda i, be, nv: (l, be[i], 0, 0)
    grid_spec = pltpu.PrefetchScalarGridSpec(
        num_scalar_prefetch=2,
        grid=(nb,),
        in_specs=[pl.BlockSpec((1, 1, tmb), lambda i, be, nv: (i, 0, 0), memory_space=pltpu.SMEM),
                  pl.BlockSpec(memory_space=pl.ANY),
                  pl.BlockSpec((None, None, d, f), wmap),
                  pl.BlockSpec((None, None, d, f), wmap),
                  pl.BlockSpec((None, None, f, d), wmap),
                  pl.BlockSpec((None, None, 1, f), wmap),
                  pl.BlockSpec((None, None, 1, f), wmap),
                  pl.BlockSpec((None, None, 1, d), wmap)],
        out_specs=pl.BlockSpec((tmb, d), lambda i, be, nv: (i, 0)),
        scratch_shapes=[pltpu.VMEM((tmb, d), F32), pltpu.SemaphoreType.DMA(())],
    )
    return pl.pallas_call(
        functools.partial(_moe_kernel, tmb=tmb),
        grid_spec=grid_spec,
        out_shape=jax.ShapeDtypeStruct((nb * tmb, d), F32),
        compiler_params=_cparams(1, 48),
        name="moe",
    )(block_expert, nvalid, slot_tok, h2, wg, wl, wd, bg, bl, bd)


def _combine_kernel(dest_ref, y_hbm, gate_ref, x_ref, mod_ref, o_ref, buf, sem, *, tt):
    def issue(r, carry):
        for k in range(TOP_K):
            src = dest_ref[0, 0, r * TOP_K + k]
            pltpu.make_async_copy(y_hbm.at[pl.ds(src, 1)], buf.at[k, pl.ds(r, 1)], sem).start()
        return carry
    lax.fori_loop(0, tt, issue, 0, unroll=4)
    def drain(r, carry):
        for k in range(TOP_K):
            src = dest_ref[0, 0, r * TOP_K + k]
            pltpu.make_async_copy(y_hbm.at[pl.ds(src, 1)], buf.at[k, pl.ds(r, 1)], sem).wait()
        return carry
    lax.fori_loop(0, tt, drain, 0, unroll=4)
    gate = gate_ref[...]
    f = gate[:, 0:1] * buf[0]
    for k in range(1, TOP_K):
        f = f + gate[:, k:k + 1] * buf[k]
    o_ref[...] = x_ref[...] + mod_ref[0, 5:6, :] * f


def _combine_call(dest, y_slots, gates, x, mods, *, l, tt, nct, tpb):
    n, d = x.shape

    def mod_idx(i):
        return (l, jnp.where(i < nct, 0, 1 + (i - nct) // tpb), 0, 0)

    row = lambda i: (i, 0)
    return pl.pallas_call(
        functools.partial(_combine_kernel, tt=tt),
        grid=(n // tt,),
        in_specs=[pl.BlockSpec((1, 1, tt * TOP_K), lambda i: (i, 0, 0), memory_space=pltpu.SMEM),
                  pl.BlockSpec(memory_space=pl.ANY),
                  pl.BlockSpec((tt, LANES), row),
                  pl.BlockSpec((tt, d), row),
                  pl.BlockSpec((None, 1, 6, d), mod_idx)],
        out_specs=pl.BlockSpec((tt, d), row),
        out_shape=jax.ShapeDtypeStruct((n, d), F32),
        scratch_shapes=[pltpu.VMEM((TOP_K, tt, d), F32), pltpu.SemaphoreType.DMA(())],
        compiler_params=_cparams(1, 32),
        name="combine",
    )(dest, y_slots, gates, x, mods)


def _final_kernel(x_ref, g_ref, o_ref):
    o_ref[...] = _rms(x_ref[...]) * g_ref[...]


def _final_call(x, gain, *, tm, nct, nl):
    d = x.shape[1]
    return pl.pallas_call(
        _final_kernel,
        grid=(nl // tm,),
        in_specs=[pl.BlockSpec((tm, d), lambda i: (nct + i, 0)), pl.BlockSpec((1, d), lambda i: (0, 0))],
        out_specs=pl.BlockSpec((tm, d), lambda i: (i, 0)),
        out_shape=jax.ShapeDtypeStruct((nl, d), F32),
        compiler_params=_cparams(1, 32),
        name="final_norm",
    )(x, gain.reshape(1, d))


def _route(logits, tmb, n_blocks):
    n = logits.shape[0]
    top_val, top_idx = lax.top_k(logits[:, :N_EXPERTS], TOP_K)
    gates = jax.nn.softmax(top_val, axis=-1)
    onehot = jnp.sum(jax.nn.one_hot(top_idx, N_EXPERTS, dtype=jnp.int32), axis=1)
    before = jnp.cumsum(onehot, axis=0) - onehot
    counts = before[-1] + onehot[-1]
    padded = (counts + tmb - 1) // tmb * tmb
    ends = jnp.cumsum(padded)
    dest = (ends - padded)[top_idx] + jnp.take_along_axis(before, top_idx, axis=1)
    tok = jnp.broadcast_to(jnp.arange(n, dtype=jnp.int32)[:, None], (n, TOP_K))
    slot_tok = jnp.zeros((n_blocks * tmb,), jnp.int32).at[dest.reshape(-1)].set(tok.reshape(-1))
    nvalid = (ends[-1] // tmb).astype(jnp.int32)
    blk = jnp.arange(n_blocks, dtype=jnp.int32)
    blk = jnp.minimum(blk, nvalid - 1)
    block_expert = jnp.minimum(jnp.searchsorted(ends, blk * tmb, side='right'), N_EXPERTS - 1).astype(jnp.int32)
    return gates, dest.astype(jnp.int32), slot_tok, block_expert, nvalid.reshape(1)


def _rope_tables(t, tm):
    rows = t // GRID_W
    row = jnp.repeat(jnp.arange(rows, dtype=F32), GRID_W)
    col = jnp.tile(jnp.arange(GRID_W, dtype=F32), rows)
    npair = HEAD_DIM // 4
    inv_freq = ROPE_THETA ** (-jnp.arange(npair, dtype=F32) / npair)
    ang = jnp.concatenate([row[:, None] * inv_freq, col[:, None] * inv_freq], axis=-1)
    cos = jnp.tile(jnp.cos(ang), (1, 4))
    sin = jnp.tile(jnp.concatenate([-jnp.sin(ang), jnp.sin(ang)], axis=-1), (1, 2))
    cos = jnp.concatenate([jnp.ones((tm, LANES), F32), cos], axis=0)
    sin = jnp.concatenate([jnp.zeros((tm, LANES), F32), sin], axis=0)
    return cos, sin


def _tile(nc, t, cap):
    for cand in (1024, 512, 256, 128):
        if cand <= cap and nc % cand == 0 and t % cand == 0:
            return cand
    raise ValueError("unsupported sequence lengths")


def kernel(x, c, ctx, c_ctx, w_ada, b_ada, w_in, b_in, q_norm, k_norm, gmlp_norm, gmlp_ws, gmlp_b, mix_norm, w_out,
           router_w, router_b, w_gate_up, b_gate_up, w_down, b_down, final_norm):
    b, t, d = x.shape
    tc = ctx.shape[1]
    depth = w_in.shape[0]
    nc, nl = b * tc, b * t
    n = nc + nl
    tm = _tile(nc, t, 512)
    rs = _tile(tc, t, 256)
    tq = _tile(tc, t, 256)
    tk = _tile(t, t, 512)
    tmb = 512
    tt = _tile(nc, t, 256)
    nct, tpb = nc // tm, t // tm
    n_blocks = -(-n * TOP_K // tmb) + N_EXPERTS

    def cols(a):
        pad = jnp.zeros(a.shape[:-1] + (LANES - 16,), a.dtype)
        return jnp.concatenate([a[..., 0:1024], a[..., 1040:2320], a[..., 1024:1040], pad], axis=-1)

    w_in_r = cols(w_in).astype(BF16)
    b_in_r = cols(b_in).reshape(depth, 1, C_END)
    qn = jnp.tile(q_norm, (1, ATTN_HEADS)).reshape(depth, 1, 512)
    kn = jnp.tile(k_norm, (1, 2)).reshape(depth, 1, LANES)
    gn = gmlp_norm.reshape(depth, 1, 256)
    ws4 = gmlp_ws.reshape(depth, GMLP_GROUPS * GMLP_CHUNK, GMLP_CHUNK).astype(BF16)
    gb = jnp.repeat(jnp.swapaxes(gmlp_b, 1, 2), HEAD_DIM, axis=2)
    mn = mix_norm.reshape(depth, 1, d)
    wo = w_out.astype(BF16)
    rw = jnp.pad(router_w, ((0, 0), (0, 0), (0, LANES - N_EXPERTS)))
    rwh = rw.astype(BF16)
    rwl = (rw - rwh.astype(F32)).astype(BF16)
    rb = jnp.pad(router_b, ((0, 0), (0, LANES - N_EXPERTS))).reshape(depth, 1, LANES)
    wg = w_gate_up[..., 0::2].astype(BF16)
    wl = w_gate_up[..., 1::2].astype(BF16)
    wd = w_down.astype(BF16)
    ne, ff = wg.shape[1], wg.shape[-1]
    bg = b_gate_up[..., 0::2].reshape(depth, ne, 1, ff)
    bl = b_gate_up[..., 1::2].reshape(depth, ne, 1, ff)
    bd = b_down.reshape(depth, ne, 1, d)
    cos_t, sin_t = _rope_tables(t, tm)

    r = -(-(b + 1) // 8) * 8
    cs = jnp.zeros((r, d), F32).at[0].set(c_ctx).at[1:b + 1].set(c)
    mods = _ada_call(cs, w_ada, b_ada).reshape(depth, r, 6, d)

    xs = jnp.concatenate([ctx.reshape(nc, d), x.reshape(nl, d)], axis=0)
    for l in range(depth):
        mq, mk, mv, mo, g, gm, aq, akd, avd = _proj_call(
            xs, mods, w_in_r, b_in_r, qn, kn, gn, ws4, gb, cos_t, sin_t, l=l, tm=tm, nct=nct, tpb=tpb)
        gt2 = jnp.pad(g[:, :16].T.reshape(16, n // MLSTM_CHUNK, MLSTM_CHUNK),
                      ((0, 0), (0, 0), (0, LANES - MLSTM_CHUNK))).reshape(16, n // MLSTM_CHUNK * LANES)
        hf, hb = _mlstm_call(mq, mk, mv, g, gt2, b=b, rs=rs, nsc=tc // rs, nsl=t // rs, ncb=nc // rs)
        kd = jnp.concatenate([akd[:nc].reshape(b, tc, 256), akd[nc:].reshape(b, t, 256)], axis=1)
        vd = jnp.concatenate([avd[:nc].reshape(b, tc, 512), avd[nc:].reshape(b, t, 512)], axis=1)
        a = _attn_call(aq, kd, vd, b=b, tq=tq, tc=tc, t=t, tk=tk, ncq=nc // tq)
        xs, h2, logits = _merge_call(hf, hb, mo, gm, a, xs, mods, mn, wo, rwh, rwl, rb, l=l, tm=tm, nct=nct, tpb=tpb)
        gates, dest, slot_tok, block_expert, nvalid = _route(logits, tmb, n_blocks)
        y_slots = _moe_call(block_expert, nvalid, slot_tok.reshape(n_blocks, 1, tmb), h2, wg, wl, wd, bg, bl, bd,
                            l=l, tmb=tmb)
        xs = _combine_call(dest.reshape(n // tt, 1, tt * TOP_K), y_slots,
                           jnp.pad(gates, ((0, 0), (0, LANES - TOP_K))), xs, mods, l=l, tt=tt,
                           nct=nc // tt, tpb=t // tt)
    out = _final_call(xs, final_norm, tm=tm, nct=nct, nl=nl)
    return out.reshape(b, t, d)
```

```python
import functools

import jax
import jax.numpy as jnp
from jax import lax
from jax.experimental import pallas as pl
from jax.experimental.pallas import tpu as pltpu

F32 = jnp.float32
BF16 = jnp.bfloat16

HEAD_DIM = 64
EPS = 1e-6
GRID_W = 64
ROPE_THETA = 10000.0
MLSTM_HEADS = 4
MLSTM_CHUNK = 64
GMLP_GROUPS = 4
GMLP_CHUNK = 128
ATTN_HEADS = 8
ATTN_KV_HEADS = 2
N_EXPERTS = 32
TOP_K = 4
SWIGLU_LIMIT = 7.0
SWIGLU_ALPHA = 1.702
LANES = 128
NEG_BIG = -1e30

C_MQ, C_MK, C_MV, C_MO, C_GU, C_GV, C_AQ, C_AK, C_AV, C_G, C_END = (
    0, 256, 512, 768, 1024, 1280, 1536, 2048, 2176, 2304, 2432)


def _dot(a, b):
    return jnp.dot(a, b, preferred_element_type=F32)


def _dot_nt(a, b):
    return lax.dot_general(a, b, (((1,), (1,)), ((), ())), preferred_element_type=F32)


def _split(x):
    hi = x.astype(BF16)
    lo = (x - hi.astype(F32)).astype(BF16)
    return hi, lo


def _cparams(ndims, vmem_mb):
    return pltpu.CompilerParams(dimension_semantics=("arbitrary",) * ndims,
                                vmem_limit_bytes=vmem_mb << 20)


def _sigmoid(x):
    return 1.0 / (1.0 + jnp.exp(-x))


def _gelu_tanh(x):
    return 0.5 * x * (1.0 + jnp.tanh(0.7978845608028654 * (x + 0.044715 * (x * x * x))))


def _rms(x):
    return x * lax.rsqrt(jnp.mean(x * x, axis=-1, keepdims=True) + EPS)


def _seg_indicator(n):
    r = lax.broadcasted_iota(jnp.int32, (n, n), 0) // HEAD_DIM
    c = lax.broadcasted_iota(jnp.int32, (n, n), 1) // HEAD_DIM
    return jnp.where(r == c, 1.0, 0.0).astype(BF16)


def _head_rms(x, ind):
    hi, lo = _split(x * x)
    ss = _dot(hi, ind) + _dot(lo, ind)
    return x * lax.rsqrt(ss * (1.0 / HEAD_DIM) + EPS)


def _ada_kernel(c_ref, w_ref, b_ref, o_ref):
    c = c_ref[...]
    s = c * _sigmoid(c)
    s_hi, s_lo = _split(s)
    w_hi, w_lo = _split(w_ref[0])
    o_ref[0] = _dot(s_hi, w_hi) + _dot(s_lo, w_hi) + _dot(s_hi, w_lo) + b_ref[0]


def _ada_call(cs, w_ada, b_ada):
    depth, d, d6 = w_ada.shape
    r = cs.shape[0]
    tn = 1536
    return pl.pallas_call(
        _ada_kernel,
        grid=(depth, d6 // tn),
        in_specs=[pl.BlockSpec((r, d), lambda l, j: (0, 0)),
                  pl.BlockSpec((1, d, tn), lambda l, j: (l, 0, j)),
                  pl.BlockSpec((1, 1, tn), lambda l, j: (l, 0, j))],
        out_specs=pl.BlockSpec((1, r, tn), lambda l, j: (l, 0, j)),
        out_shape=jax.ShapeDtypeStruct((depth, r, d6), F32),
        compiler_params=_cparams(2, 40),
        name="ada",
    )(cs, w_ada, b_ada.reshape(depth, 1, d6))


def _proj_kernel(x_ref, mod_ref, w_ref, b_ref, qn_ref, kn_ref, gn_ref, ws_ref, gb_ref, cos_ref, sin_ref,
                 mq_ref, mk_ref, mv_ref, mo_ref, g_ref, gm_ref, aq_ref, ak_ref, av_ref, *, tm):
    h = (_rms(x_ref[...]) * (1.0 + mod_ref[0, 1:2, :]) + mod_ref[0, 0:1, :]).astype(BF16)

    def seg(a, b):
        return _dot(h, w_ref[:, a:b]) + b_ref[:, a:b]

    mq_ref[...] = seg(C_MQ, C_MK).astype(BF16)
    mk_ref[...] = (seg(C_MK, C_MV) * HEAD_DIM ** -0.5).astype(BF16)
    mv_ref[...] = seg(C_MV, C_MO).astype(BF16)
    mo_ref[...] = seg(C_MO, C_GU)

    g = seg(C_G, C_END)
    ls = jnp.minimum(g, 0.0) - jnp.log1p(jnp.exp(-jnp.abs(g)))
    r = lax.broadcasted_iota(jnp.int32, (tm, tm), 0)
    c = lax.broadcasted_iota(jnp.int32, (tm, tm), 1)
    same = (r // MLSTM_CHUNK) == (c // MLSTM_CHUNK)
    pre = jnp.where(same, jnp.where(c <= r, 1.0, 0.0), 0.0).astype(BF16)
    suf = jnp.where(same, jnp.where(c >= r, 1.0, 0.0), 0.0).astype(BF16)
    ls_hi, ls_lo = _split(ls)
    psum = _dot(pre, ls_hi) + _dot(pre, ls_lo)
    ssum = _dot(suf, ls_hi) + _dot(suf, ls_lo)
    lane = lax.broadcasted_iota(jnp.int32, (tm, LANES), 1)
    q4 = lane // MLSTM_HEADS
    g_ref[...] = jnp.where(q4 == 1, psum, jnp.where(q4 == 3, ssum, g))

    u = _gelu_tanh(seg(C_GU, C_GV))
    v = (_rms(_gelu_tanh(seg(C_GV, C_AQ))) * gn_ref[...]).astype(BF16)
    grp = lax.broadcasted_iota(jnp.int32, (GMLP_CHUNK, 2 * LANES), 1) // HEAD_DIM
    for ci in range(tm // GMLP_CHUNK):
        rows = slice(ci * GMLP_CHUNK, (ci + 1) * GMLP_CHUNK)
        full = _dot(ws_ref[...], v[rows, :])
        mixed = gb_ref[...]
        for gi in range(GMLP_GROUPS):
            mixed = mixed + jnp.where(grp == gi, full[gi * GMLP_CHUNK:(gi + 1) * GMLP_CHUNK, :], 0.0)
        gm_ref[rows, :] = u[rows, :] * mixed

    cos = cos_ref[...]
    sin = sin_ref[...]
    first_half = (lane % HEAD_DIM) < (HEAD_DIM // 2)
    lo_half = lane < HEAD_DIM

    def rope(blk):
        partner = jnp.where(first_half, pltpu.roll(blk, LANES - HEAD_DIM // 2, 1),
                            pltpu.roll(blk, HEAD_DIM // 2, 1))
        return blk * cos + partner * sin

    ind256 = _seg_indicator(2 * LANES)
    for s in range(2):
        qn = _head_rms(seg(C_AQ + s * 256, C_AQ + (s + 1) * 256), ind256) * qn_ref[:, s * 256:(s + 1) * 256]
        for j in range(2):
            blk = rope(qn[:, j * LANES:(j + 1) * LANES]) * HEAD_DIM ** -0.5
            aq_ref[:, s * 256 + j * LANES: s * 256 + (j + 1) * LANES] = blk.astype(BF16)

    kr = rope(_head_rms(seg(C_AK, C_AV), ind256[:LANES, :LANES]) * kn_ref[...])
    kroll = pltpu.roll(kr, HEAD_DIM, 1)
    ak_ref[:, 0:LANES] = jnp.where(lo_half, kr, kroll).astype(BF16)
    ak_ref[:, LANES:2 * LANES] = jnp.where(lo_half, kroll, kr).astype(BF16)

    vf = seg(C_AV, C_G)
    vroll = pltpu.roll(vf, HEAD_DIM, 1)
    av_ref[:, 0:LANES] = jnp.where(lo_half, vf, 1.0).astype(BF16)
    av_ref[:, LANES:2 * LANES] = jnp.where(lo_half, 1.0, vroll).astype(BF16)
    av_ref[:, 2 * LANES:3 * LANES] = jnp.where(lo_half, vroll, 1.0).astype(BF16)
    av_ref[:, 3 * LANES:4 * LANES] = jnp.where(lo_half, 1.0, vf).astype(BF16)


def _proj_call(x, mods, w, b, qn, kn, gn, ws4, gb, cos_t, sin_t, *, l, tm, nct, tpb):
    n, d = x.shape

    def mod_idx(i):
        return (l, jnp.where(i < nct, 0, 1 + (i - nct) // tpb), 0, 0)

    def rope_idx(i):
        return (jnp.where(i < nct, 0, 1 + (i - nct) % tpb), 0)

    row = lambda i: (i, 0)
    const2 = lambda i: (0, 0)
    lay = lambda i: (l, 0, 0)
    outs = [(256, BF16), (256, BF16), (256, BF16), (256, F32), (LANES, F32), (256, F32),
            (512, BF16), (256, BF16), (512, BF16)]
    return pl.pallas_call(
        functools.partial(_proj_kernel, tm=tm),
        grid=(n // tm,),
        in_specs=[pl.BlockSpec((tm, d), row),
                  pl.BlockSpec((None, 1, 6, d), mod_idx),
                  pl.BlockSpec((None, d, C_END), lay),
                  pl.BlockSpec((None, 1, C_END), lay),
                  pl.BlockSpec((None, 1, 512), lay),
                  pl.BlockSpec((None, 1, LANES), lay),
                  pl.BlockSpec((None, 1, 256), lay),
                  pl.BlockSpec((None, GMLP_GROUPS * GMLP_CHUNK, GMLP_CHUNK), lay),
                  pl.BlockSpec((None, GMLP_CHUNK, 256), lay),
                  pl.BlockSpec((tm, LANES), rope_idx),
                  pl.BlockSpec((tm, LANES), rope_idx)],
        out_specs=[pl.BlockSpec((tm, wd), row) for wd, _ in outs],
        out_shape=[jax.ShapeDtypeStruct((n, wd), dt) for wd, dt in outs],
        compiler_params=_cparams(1, 48),
        name="proj",
    )(x, mods, w, b, qn, kn, gn, ws4, gb, cos_t, sin_t)


def _mlstm_kernel(kf_ref, qf_ref, vf_ref, gf_ref, gtf_ref, kb_ref, qb_ref, vb_ref, gb_ref, gtb_ref,
                  hf_ref, hb_ref, c_ref, m_ref, *, nch):
    ch = MLSTM_CHUNK

    @pl.when(pl.program_id(1) == 0)
    def _():
        c_ref[...] = jnp.zeros_like(c_ref)
        m_ref[...] = jnp.zeros_like(m_ref)

    sub = lax.broadcasted_iota(jnp.int32, (ch, LANES), 0)
    lane = lax.broadcasted_iota(jnp.int32, (ch, LANES), 1)
    lo_half = lane < ch
    pos = lane % ch
    r2 = lax.broadcasted_iota(jnp.int32, (LANES, LANES), 0)
    c2 = lax.broadcasted_iota(jnp.int32, (LANES, LANES), 1)
    blockdiag = (r2 < ch) == (c2 < ch)
    ones_rows = jnp.ones((ch, LANES), BF16)
    dirs = ((kf_ref, qf_ref, vf_ref, gf_ref, gtf_ref, hf_ref), (kb_ref, qb_ref, vb_ref, gb_ref, gtb_ref, hb_ref))
    for d, (k_ref, q_ref, v_ref, g_ref, gt_ref, h_ref) in enumerate(dirs):
        mask = (sub <= pos) if d == 0 else (sub >= pos)
        sels = []
        for p in range(MLSTM_HEADS // 2):
            gi = d * 2 * MLSTM_HEADS + 2 * p + jnp.where(c2 >= ch, 1, 0)
            sels.append((jnp.where(r2 == gi, 1.0, 0.0) - jnp.where(r2 == gi + MLSTM_HEADS, 1.0, 0.0)).astype(BF16))
        order = range(nch) if d == 0 else range(nch - 1, -1, -1)
        for ci in order:
            rows = slice(ci * ch, (ci + 1) * ch)
            cl = slice(ci * LANES, (ci + 1) * LANES)
            g = g_ref[rows, :]
            g_hi = g.astype(BF16)
            g_r = g - g_hi.astype(F32)
            g_mid = g_r.astype(BF16)
            g_lo = (g_r - g_mid.astype(F32)).astype(BF16)
            hts = []
            for p in range(MLSTM_HEADS // 2):
                st = d * (MLSTM_HEADS // 2) + p
                kp = k_ref[rows, p * LANES:(p + 1) * LANES]
                zero = jnp.zeros_like(kp)
                qbd = jnp.where(blockdiag, q_ref[p * LANES:(p + 1) * LANES, cl], jnp.zeros((LANES, LANES), BF16))
                vat = jnp.concatenate([v_ref[p, :, cl], ones_rows], axis=0)
                irow = gt_ref[3 * st:3 * st + 1, cl]
                brow = gt_ref[3 * st + 1:3 * st + 2, cl]
                bend = gt_ref[3 * st + 2:3 * st + 3, cl]
                x = _dot(g_hi, sels[p]) + _dot(g_mid, sels[p]) + _dot(g_lo, sels[p])
                dmt = jnp.where(mask, brow + x, -jnp.inf)
                cmax = jnp.max(dmt, axis=0, keepdims=True)
                p0 = (_dot(kp, qbd) * jnp.exp(dmt - cmax)).astype(BF16)
                rt0 = _dot(vat, jnp.concatenate([jnp.where(lo_half, p0, zero), jnp.where(lo_half, zero, p0)], axis=0))
                dmax = jnp.max(bend + x, axis=0, keepdims=True)
                vw = (vat.astype(F32) * jnp.exp(bend + irow - brow - dmax)).astype(BF16)
                u0 = _dot(vw, jnp.concatenate([jnp.where(lo_half, kp, zero), jnp.where(lo_half, zero, kp)], axis=0))
                m = m_ref[st][0:1, :]
                cst = c_ref[st]
                inter = brow + m
                mt = jnp.maximum(inter, cmax)
                rt = jnp.exp(cmax - mt) * rt0 + jnp.exp(inter - mt) * _dot(cst.astype(BF16), qbd)
                hts.append(rt[0:ch] / jnp.maximum(jnp.abs(rt[ch:2 * ch]), jnp.exp(-mt)))
                mnew = jnp.maximum(bend + m, dmax)
                c_ref[st] = jnp.exp(bend + m - mnew) * cst + jnp.exp(dmax - mnew) * u0
                m_ref[st] = jnp.broadcast_to(mnew, (8, LANES))
            ht = jnp.concatenate(hts, axis=0).T
            h_ref[rows, 0:LANES] = ht[0:ch]
            h_ref[rows, LANES:2 * LANES] = ht[ch:2 * ch]


def _mlstm_layouts(mq, mv, g):
    n = mq.shape[0]
    nchunk = n // MLSTM_CHUNK
    qt = mq.T.reshape(256, nchunk, MLSTM_CHUNK)
    qtd = jnp.concatenate([qt, qt], axis=-1).reshape(256, nchunk * LANES)
    vt2 = mv.T.reshape(2, 2, HEAD_DIM, nchunk, MLSTM_CHUNK).transpose(0, 2, 3, 1, 4).reshape(2, HEAD_DIM, nchunk * LANES)
    gc = g[:, :16].reshape(nchunk, MLSTM_CHUNK, 16)
    rows = []
    for d in range(2):
        for p in range(2):
            ci = d * 8 + 2 * p
            irow = jnp.concatenate([gc[:, :, ci], gc[:, :, ci + 1]], axis=-1)
            brow = jnp.concatenate([gc[:, :, ci + 4], gc[:, :, ci + 5]], axis=-1)
            edge = MLSTM_CHUNK - 1 if d == 0 else 0
            bend = jnp.concatenate([jnp.broadcast_to(gc[:, edge, ci + 4 + e][:, None], (nchunk, MLSTM_CHUNK))
                                    for e in range(2)], axis=-1)
            rows += [irow, brow, bend]
    rows += [jnp.zeros_like(rows[0])] * 4
    return qtd, vt2, jnp.stack(rows, axis=0).reshape(16, nchunk * LANES)


def _mlstm_call(mk, qtd, vt2, g, gt3, *, b, rs, nsc, nsl, ncb):
    n = mk.shape[0]
    nch = rs // MLSTM_CHUNK

    def fwd(bi, j):
        return jnp.where(j < nsc, bi * nsc + j, ncb + bi * nsl + (j - nsc))

    def bwd(bi, j):
        return jnp.where(j < nsc, bi * nsc + (nsc - 1 - j), ncb + bi * nsl + (nsl - 1 - (j - nsc)))

    def specs(idx):
        return [pl.BlockSpec((rs, 256), lambda bi, j: (idx(bi, j), 0)),
                pl.BlockSpec((256, nch * LANES), lambda bi, j: (0, idx(bi, j))),
                pl.BlockSpec((2, HEAD_DIM, nch * LANES), lambda bi, j: (0, 0, idx(bi, j))),
                pl.BlockSpec((rs, LANES), lambda bi, j: (idx(bi, j), 0)),
                pl.BlockSpec((16, nch * LANES), lambda bi, j: (0, idx(bi, j)))]

    return pl.pallas_call(
        functools.partial(_mlstm_kernel, nch=nch),
        grid=(b, nsc + nsl),
        in_specs=specs(fwd) + specs(bwd),
        out_specs=[pl.BlockSpec((rs, 256), lambda bi, j: (fwd(bi, j), 0)),
                   pl.BlockSpec((rs, 256), lambda bi, j: (bwd(bi, j), 0))],
        out_shape=[jax.ShapeDtypeStruct((n, 256), F32)] * 2,
        scratch_shapes=[pltpu.VMEM((MLSTM_HEADS, LANES, LANES), F32),
                        pltpu.VMEM((MLSTM_HEADS, 8, LANES), F32)],
        compiler_params=_cparams(2, 32),
        name="mlstm",
    )(mk, qtd, vt2, g, gt3, mk, qtd, vt2, g, gt3)


def _attn_kernel(q_ref, k_ref, v_ref, o_ref, qs_ref, m_ref, aa_ref, ab_ref, *, tq, tc, t, tk, nqc):
    lane = lax.broadcasted_iota(jnp.int32, (tq, LANES), 1)
    lo_half = lane < HEAD_DIM
    for g in range(ATTN_KV_HEADS):
        for pp in range(2):
            qp = q_ref[:, (2 * g + pp) * LANES:(2 * g + pp + 1) * LANES]
            zero = jnp.zeros_like(qp)
            qs_ref[g, pp * tq:(pp + 1) * tq, :] = jnp.where(lo_half, qp, zero)
            qs_ref[g, (2 + pp) * tq:(3 + pp) * tq, :] = jnp.where(lo_half, zero, qp)
    m_ref[...] = jnp.full_like(m_ref, NEG_BIG)
    aa_ref[...] = jnp.zeros_like(aa_ref)
    ab_ref[...] = jnp.zeros_like(ab_ref)

    def chunk(start, size):
        for g in range(ATTN_KV_HEADS):
            kc = k_ref[0, pl.ds(start, size), g * LANES:(g + 1) * LANES]
            s = _dot_nt(qs_ref[g], kc)
            m_old = m_ref[g]
            m_new = jnp.maximum(m_old, jnp.max(s, axis=-1, keepdims=True))
            alpha = jnp.exp(m_old - m_new)
            p = jnp.exp(s - jnp.concatenate([m_new] * (size // LANES), axis=1)).astype(BF16)
            va = v_ref[0, pl.ds(start, size), (2 * g) * LANES:(2 * g + 1) * LANES]
            vb = v_ref[0, pl.ds(start, size), (2 * g + 1) * LANES:(2 * g + 2) * LANES]
            aa_ref[g] = alpha[:2 * tq] * aa_ref[g] + _dot(p[:2 * tq], va)
            ab_ref[g] = alpha[2 * tq:] * ab_ref[g] + _dot(p[2 * tq:], vb)
            m_ref[g] = m_new

    for c0 in range(0, tc, tk):
        chunk(c0, min(tk, tc - c0))

    @pl.when(pl.program_id(1) >= nqc)
    def _():
        def body(i, carry):
            chunk(pl.multiple_of(tc + i * tk, LANES), tk)
            return carry
        lax.fori_loop(0, t // tk, body, 0)

    for g in range(ATTN_KV_HEADS):
        for pp in range(2):
            a = aa_ref[g, pp * tq:(pp + 1) * tq, :]
            bq = ab_ref[g, pp * tq:(pp + 1) * tq, :]
            oa = a * pltpu.roll(1.0 / a, HEAD_DIM, 1)
            ob = bq * pltpu.roll(1.0 / bq, HEAD_DIM, 1)
            o_ref[:, (2 * g + pp) * LANES:(2 * g + pp + 1) * LANES] = jnp.where(lo_half, oa, ob)


def _attn_call(aq, kd, vd, *, b, tq, tc, t, tk, ncq):
    n = aq.shape[0]
    nqc, nql = tc // tq, t // tq

    def qidx(bi, j):
        return (jnp.where(j < nqc, bi * nqc + j, ncq + bi * nql + (j - nqc)), 0)

    return pl.pallas_call(
        functools.partial(_attn_kernel, tq=tq, tc=tc, t=t, tk=tk, nqc=nqc),
        grid=(b, nqc + nql),
        in_specs=[pl.BlockSpec((tq, 512), qidx),
                  pl.BlockSpec((1, tc + t, 256), lambda bi, j: (bi, 0, 0)),
                  pl.BlockSpec((1, tc + t, 512), lambda bi, j: (bi, 0, 0))],
        out_specs=pl.BlockSpec((tq, 512), qidx),
        out_shape=jax.ShapeDtypeStruct((n, 512), F32),
        scratch_shapes=[pltpu.VMEM((ATTN_KV_HEADS, 4 * tq, LANES), BF16),
                        pltpu.VMEM((ATTN_KV_HEADS, 4 * tq, LANES), F32),
                        pltpu.VMEM((ATTN_KV_HEADS, 2 * tq, LANES), F32),
                        pltpu.VMEM((ATTN_KV_HEADS, 2 * tq, LANES), F32)],
        compiler_params=_cparams(2, 48),
        name="attn",
    )(aq, kd, vd)


def _merge_kernel(hf_ref, hb_ref, mo_ref, gm_ref, a_ref, x_ref, mod_ref, mn_ref, wo_ref, rwh_ref, rwl_ref, rb_ref,
                  xo_ref, h2_ref, lg_ref):
    ind = _seg_indicator(2 * LANES)
    parts = [_sigmoid(mo_ref[...]) * (hf_ref[...] + hb_ref[...]), gm_ref[...],
             a_ref[:, 0:256], a_ref[:, 256:512]]
    y = jnp.concatenate([_head_rms(pt, ind) for pt in parts], axis=1) * mn_ref[...]
    xn = x_ref[...] + mod_ref[0, 2:3, :] * _dot(y.astype(BF16), wo_ref[...])
    xo_ref[...] = xn
    h2 = _rms(xn) * (1.0 + mod_ref[0, 4:5, :]) + mod_ref[0, 3:4, :]
    h2_ref[...] = h2
    hi, lo = _split(h2)
    lg_ref[...] = _dot(hi, rwh_ref[...]) + _dot(lo, rwh_ref[...]) + _dot(hi, rwl_ref[...]) + rb_ref[...]


def _merge_call(hf, hb, mo, gm, a, x, mods, mn, wo, rwh, rwl, rb, *, l, tm, nct, tpb):
    n, d = x.shape

    def mod_idx(i):
        return (l, jnp.where(i < nct, 0, 1 + (i - nct) // tpb), 0, 0)

    row = lambda i: (i, 0)
    lay = lambda i: (l, 0, 0)
    return pl.pallas_call(
        _merge_kernel,
        grid=(n // tm,),
        in_specs=[pl.BlockSpec((tm, 256), row), pl.BlockSpec((tm, 256), row), pl.BlockSpec((tm, 256), row),
                  pl.BlockSpec((tm, 256), row), pl.BlockSpec((tm, 512), row), pl.BlockSpec((tm, d), row),
                  pl.BlockSpec((None, 1, 6, d), mod_idx),
                  pl.BlockSpec((None, 1, d), lay),
                  pl.BlockSpec((None, d, d), lay),
                  pl.BlockSpec((None, d, LANES), lay),
                  pl.BlockSpec((None, d, LANES), lay),
                  pl.BlockSpec((None, 1, LANES), lay)],
        out_specs=[pl.BlockSpec((tm, d), row), pl.BlockSpec((tm, d), row), pl.BlockSpec((tm, LANES), row)],
        out_shape=[jax.ShapeDtypeStruct((n, d), F32), jax.ShapeDtypeStruct((n, d), F32),
                   jax.ShapeDtypeStruct((n, LANES), F32)],
        compiler_params=_cparams(1, 48),
        name="merge",
    )(hf, hb, mo, gm, a, x, mods, mn, wo, rwh, rwl, rb)


def _wprep_kernel(w_ref, o_ref, *, f):
    slab = 2 * LANES
    r = lax.broadcasted_iota(jnp.int32, (slab, slab), 0)
    c = lax.broadcasted_iota(jnp.int32, (slab, slab), 1)
    perm = jnp.where(r == jnp.where(c < LANES, 2 * c, 2 * (c - LANES) + 1), 1.0, 0.0).astype(BF16)
    for s in range(2 * f // slab):
        y = _dot(w_ref[:, s * slab:(s + 1) * slab].astype(BF16), perm).astype(BF16)
        o_ref[:, s * LANES:(s + 1) * LANES] = y[:, :LANES]
        o_ref[:, f + s * LANES:f + (s + 1) * LANES] = y[:, LANES:]


def _wprep_call(w_gate_up):
    depth, ne, d, f2 = w_gate_up.shape
    rows = 512
    w2 = w_gate_up.reshape(depth * ne * d, f2)
    out = pl.pallas_call(
        functools.partial(_wprep_kernel, f=f2 // 2),
        grid=(depth * ne * d // rows,),
        in_specs=[pl.BlockSpec((rows, f2), lambda i: (i, 0))],
        out_specs=pl.BlockSpec((rows, f2), lambda i: (i, 0)),
        out_shape=jax.ShapeDtypeStruct((depth * ne * d, f2), BF16),
        compiler_params=_cparams(1, 32),
        name="wprep",
    )(w2)
    return out.reshape(depth, ne, d, f2)


def _moe_kernel(be_ref, nv_ref, tok_ref, h_hbm, wgl_ref, wd_ref, bgl_ref, bd_ref, y_ref, xbuf, sem, *, tmb, f):
    i = pl.program_id(0)

    @pl.when(i < nv_ref[0])
    def _():
        def issue(r, carry):
            tok = tok_ref[0, 0, r]
            pltpu.make_async_copy(h_hbm.at[pl.ds(tok, 1)], xbuf.at[pl.ds(r, 1)], sem).start()
            return carry
        lax.fori_loop(0, tmb, issue, 0, unroll=8)
        def drain(r, carry):
            tok = tok_ref[0, 0, r]
            pltpu.make_async_copy(h_hbm.at[pl.ds(tok, 1)], xbuf.at[pl.ds(r, 1)], sem).wait()
            return carry
        lax.fori_loop(0, tmb, drain, 0, unroll=8)
        xb = xbuf[...].astype(BF16)
        gl = _dot(xb, wgl_ref[...]) + bgl_ref[...]
        x_glu = jnp.minimum(gl[:, :f], SWIGLU_LIMIT)
        x_lin = jnp.clip(gl[:, f:], -SWIGLU_LIMIT, SWIGLU_LIMIT)
        act = x_glu * _sigmoid(SWIGLU_ALPHA * x_glu) * (x_lin + 1.0)
        y_ref[...] = _dot(act.astype(BF16), wd_ref[...]) + bd_ref[...]

    @pl.when(i >= nv_ref[0])
    def _():
        y_ref[...] = jnp.zeros_like(y_ref)


def _moe_call(block_expert, nvalid, slot_tok, h2, wgl, wd, bgl, bd, *, l, tmb):
    n, d = h2.shape
    nb = block_expert.shape[0]
    f = wd.shape[-2]
    wmap = lambda i, be, nv: (l, be[i], 0, 0)
    grid_spec = pltpu.PrefetchScalarGridSpec(
        num_scalar_prefetch=2,
        grid=(nb,),
        in_specs=[pl.BlockSpec((1, 1, tmb), lambda i, be, nv: (i, 0, 0), memory_space=pltpu.SMEM),
                  pl.BlockSpec(memory_space=pl.ANY),
                  pl.BlockSpec((None, None, d, 2 * f), wmap),
                  pl.BlockSpec((None, None, f, d), wmap),
                  pl.BlockSpec((None, None, 1, 2 * f), wmap),
                  pl.BlockSpec((None, None, 1, d), wmap)],
        out_specs=pl.BlockSpec((tmb, d), lambda i, be, nv: (i, 0)),
        scratch_shapes=[pltpu.VMEM((tmb, d), F32), pltpu.SemaphoreType.DMA(())],
    )
    return pl.pallas_call(
        functools.partial(_moe_kernel, tmb=tmb, f=f),
        grid_spec=grid_spec,
        out_shape=jax.ShapeDtypeStruct((nb * tmb, d), F32),
        compiler_params=_cparams(1, 48),
        name="moe",
    )(block_expert, nvalid, slot_tok, h2, wgl, wd, bgl, bd)


def _combine_kernel(dest_ref, y_hbm, gate_ref, x_ref, mod_ref, o_ref, buf, sem, *, tt):
    def issue(r, carry):
        for k in range(TOP_K):
            src = dest_ref[0, 0, r * TOP_K + k]
            pltpu.make_async_copy(y_hbm.at[pl.ds(src, 1)], buf.at[k, pl.ds(r, 1)], sem).start()
        return carry
    lax.fori_loop(0, tt, issue, 0, unroll=4)
    def drain(r, carry):
        for k in range(TOP_K):
            src = dest_ref[0, 0, r * TOP_K + k]
            pltpu.make_async_copy(y_hbm.at[pl.ds(src, 1)], buf.at[k, pl.ds(r, 1)], sem).wait()
        return carry
    lax.fori_loop(0, tt, drain, 0, unroll=4)
    gate = gate_ref[...]
    f = gate[:, 0:1] * buf[0]
    for k in range(1, TOP_K):
        f = f + gate[:, k:k + 1] * buf[k]
    o_ref[...] = x_ref[...] + mod_ref[0, 5:6, :] * f


def _combine_call(dest, y_slots, gates, x, mods, *, l, tt, nct, tpb):
    n, d = x.shape

    def mod_idx(i):
        return (l, jnp.where(i < nct, 0, 1 + (i - nct) // tpb), 0, 0)

    row = lambda i: (i, 0)
    return pl.pallas_call(
        functools.partial(_combine_kernel, tt=tt),
        grid=(n // tt,),
        in_specs=[pl.BlockSpec((1, 1, tt * TOP_K), lambda i: (i, 0, 0), memory_space=pltpu.SMEM),
                  pl.BlockSpec(memory_space=pl.ANY),
                  pl.BlockSpec((tt, LANES), row),
                  pl.BlockSpec((tt, d), row),
                  pl.BlockSpec((None, 1, 6, d), mod_idx)],
        out_specs=pl.BlockSpec((tt, d), row),
        out_shape=jax.ShapeDtypeStruct((n, d), F32),
        scratch_shapes=[pltpu.VMEM((TOP_K, tt, d), F32), pltpu.SemaphoreType.DMA(())],
        compiler_params=_cparams(1, 32),
        name="combine",
    )(dest, y_slots, gates, x, mods)


def _final_kernel(x_ref, g_ref, o_ref):
    o_ref[...] = _rms(x_ref[...]) * g_ref[...]


def _final_call(x, gain, *, tm, nct, nl):
    d = x.shape[1]
    return pl.pallas_call(
        _final_kernel,
        grid=(nl // tm,),
        in_specs=[pl.BlockSpec((tm, d), lambda i: (nct + i, 0)), pl.BlockSpec((1, d), lambda i: (0, 0))],
        out_specs=pl.BlockSpec((tm, d), lambda i: (i, 0)),
        out_shape=jax.ShapeDtypeStruct((nl, d), F32),
        compiler_params=_cparams(1, 32),
        name="final_norm",
    )(x, gain.reshape(1, d))


def _route(logits, tmb, n_blocks):
    n = logits.shape[0]
    top_val, top_idx = lax.top_k(logits[:, :N_EXPERTS], TOP_K)
    gates = jax.nn.softmax(top_val, axis=-1)
    onehot = jnp.sum(jax.nn.one_hot(top_idx, N_EXPERTS, dtype=jnp.int32), axis=1)
    before = jnp.cumsum(onehot, axis=0) - onehot
    counts = before[-1] + onehot[-1]
    padded = (counts + tmb - 1) // tmb * tmb
    ends = jnp.cumsum(padded)
    dest = (ends - padded)[top_idx] + jnp.take_along_axis(before, top_idx, axis=1)
    tok = jnp.broadcast_to(jnp.arange(n, dtype=jnp.int32)[:, None], (n, TOP_K))
    slot_tok = jnp.zeros((n_blocks * tmb,), jnp.int32).at[dest.reshape(-1)].set(tok.reshape(-1))
    nvalid = (ends[-1] // tmb).astype(jnp.int32)
    blk = jnp.arange(n_blocks, dtype=jnp.int32)
    blk = jnp.minimum(blk, nvalid - 1)
    block_expert = jnp.minimum(jnp.searchsorted(ends, blk * tmb, side='right'), N_EXPERTS - 1).astype(jnp.int32)
    return gates, dest.astype(jnp.int32), slot_tok, block_expert, nvalid.reshape(1)


def _rope_tables(t, tm):
    rows = t // GRID_W
    row = jnp.repeat(jnp.arange(rows, dtype=F32), GRID_W)
    col = jnp.tile(jnp.arange(GRID_W, dtype=F32), rows)
    npair = HEAD_DIM // 4
    inv_freq = ROPE_THETA ** (-jnp.arange(npair, dtype=F32) / npair)
    ang = jnp.concatenate([row[:, None] * inv_freq, col[:, None] * inv_freq], axis=-1)
    cos = jnp.tile(jnp.cos(ang), (1, 4))
    sin = jnp.tile(jnp.concatenate([-jnp.sin(ang), jnp.sin(ang)], axis=-1), (1, 2))
    cos = jnp.concatenate([jnp.ones((tm, LANES), F32), cos], axis=0)
    sin = jnp.concatenate([jnp.zeros((tm, LANES), F32), sin], axis=0)
    return cos, sin


def _tile(nc, t, cap):
    for cand in (1024, 512, 256, 128):
        if cand <= cap and nc % cand == 0 and t % cand == 0:
            return cand
    raise ValueError("unsupported sequence lengths")


def kernel(x, c, ctx, c_ctx, w_ada, b_ada, w_in, b_in, q_norm, k_norm, gmlp_norm, gmlp_ws, gmlp_b, mix_norm, w_out,
           router_w, router_b, w_gate_up, b_gate_up, w_down, b_down, final_norm):
    b, t, d = x.shape
    tc = ctx.shape[1]
    depth = w_in.shape[0]
    nc, nl = b * tc, b * t
    n = nc + nl
    tm = _tile(nc, t, 512)
    rs = _tile(tc, t, 256)
    tq = _tile(tc, t, 256)
    tk = _tile(t, t, 512)
    tmb = 512
    tt = _tile(nc, t, 256)
    nct, tpb = nc // tm, t // tm
    n_blocks = -(-n * TOP_K // tmb) + N_EXPERTS

    def heads_0213(a, axis):
        blk = [lax.slice_in_dim(a, i * HEAD_DIM, (i + 1) * HEAD_DIM, axis=axis) for i in (0, 2, 1, 3)]
        rest = lax.slice_in_dim(a, MLSTM_HEADS * HEAD_DIM, a.shape[axis], axis=axis)
        return jnp.concatenate(blk + [rest], axis=axis)

    def cols(a):
        pad = jnp.zeros(a.shape[:-1] + (LANES - 16,), a.dtype)
        return jnp.concatenate([a[..., 0:768], heads_0213(a[..., 768:1024], a.ndim - 1), a[..., 1040:2320],
                                a[..., 1024:1040], pad], axis=-1)

    w_in_r = cols(w_in).astype(BF16)
    b_in_r = cols(b_in).reshape(depth, 1, C_END)
    qn = jnp.tile(q_norm, (1, ATTN_HEADS)).reshape(depth, 1, 512)
    kn = jnp.tile(k_norm, (1, 2)).reshape(depth, 1, LANES)
    gn = gmlp_norm.reshape(depth, 1, 256)
    ws4 = gmlp_ws.reshape(depth, GMLP_GROUPS * GMLP_CHUNK, GMLP_CHUNK).astype(BF16)
    gb = jnp.repeat(jnp.swapaxes(gmlp_b, 1, 2), HEAD_DIM, axis=2)
    mn = heads_0213(mix_norm, 1).reshape(depth, 1, d)
    wo = heads_0213(w_out, 1).astype(BF16)
    rw = jnp.pad(router_w, ((0, 0), (0, 0), (0, LANES - N_EXPERTS)))
    rwh = rw.astype(BF16)
    rwl = (rw - rwh.astype(F32)).astype(BF16)
    rb = jnp.pad(router_b, ((0, 0), (0, LANES - N_EXPERTS))).reshape(depth, 1, LANES)
    wgl = _wprep_call(w_gate_up)
    wd = w_down.astype(BF16)
    ne, ff = w_down.shape[1], w_down.shape[2]
    bgl = jnp.concatenate([b_gate_up[..., 0::2], b_gate_up[..., 1::2]], axis=-1).reshape(depth, ne, 1, 2 * ff)
    bd = b_down.reshape(depth, ne, 1, d)
    cos_t, sin_t = _rope_tables(t, tm)

    r = -(-(b + 1) // 8) * 8
    cs = jnp.zeros((r, d), F32).at[0].set(c_ctx).at[1:b + 1].set(c)
    mods = _ada_call(cs, w_ada, b_ada).reshape(depth, r, 6, d)

    xs = jnp.concatenate([ctx.reshape(nc, d), x.reshape(nl, d)], axis=0)
    for l in range(depth):
        mq, mk, mv, mo, g, gm, aq, akd, avd = _proj_call(
            xs, mods, w_in_r, b_in_r, qn, kn, gn, ws4, gb, cos_t, sin_t, l=l, tm=tm, nct=nct, tpb=tpb)
        qtd, vt2, gt3 = _mlstm_layouts(mq, mv, g)
        hf, hb = _mlstm_call(mk, qtd, vt2, g, gt3, b=b, rs=rs, nsc=tc // rs, nsl=t // rs, ncb=nc // rs)
        kd = jnp.concatenate([akd[:nc].reshape(b, tc, 256), akd[nc:].reshape(b, t, 256)], axis=1)
        vd = jnp.concatenate([avd[:nc].reshape(b, tc, 512), avd[nc:].reshape(b, t, 512)], axis=1)
        a = _attn_call(aq, kd, vd, b=b, tq=tq, tc=tc, t=t, tk=tk, ncq=nc // tq)
        xs, h2, logits = _merge_call(hf, hb, mo, gm, a, xs, mods, mn, wo, rwh, rwl, rb, l=l, tm=tm, nct=nct, tpb=tpb)
        gates, dest, slot_tok, block_expert, nvalid = _route(logits, tmb, n_blocks)
        y_slots = _moe_call(block_expert, nvalid, slot_tok.reshape(n_blocks, 1, tmb), h2, wgl, wd, bgl, bd,
                            l=l, tmb=tmb)
        xs = _combine_call(dest.reshape(n // tt, 1, tt * TOP_K), y_slots,
                           jnp.pad(gates, ((0, 0), (0, LANES - TOP_K))), xs, mods, l=l, tt=tt,
                           nct=nc // tt, tpb=t // tt)
    out = _final_call(xs, final_norm, tm=tm, nct=nct, nl=nl)
    return out.reshape(b, t, d)
```

```python
import functools

import jax
import jax.numpy as jnp
from jax import lax
from jax.experimental import pallas as pl
from jax.experimental.pallas import tpu as pltpu

F32 = jnp.float32
BF16 = jnp.bfloat16

HEAD_DIM = 64
EPS = 1e-6
GRID_W = 64
ROPE_THETA = 10000.0
MLSTM_HEADS = 4
MLSTM_CHUNK = 64
GMLP_GROUPS = 4
GMLP_CHUNK = 128
ATTN_HEADS = 8
ATTN_KV_HEADS = 2
N_EXPERTS = 32
TOP_K = 4
SWIGLU_LIMIT = 7.0
SWIGLU_ALPHA = 1.702
LANES = 128
NEG_BIG = -1e30

C_MQ, C_MK, C_MV, C_MO, C_GU, C_GV, C_AQ, C_AK, C_AV, C_G, C_END = (
    0, 256, 512, 768, 1024, 1280, 1536, 2048, 2176, 2304, 2432)


def _dot(a, b):
    return jnp.dot(a, b, preferred_element_type=F32)


def _dot_nt(a, b):
    return lax.dot_general(a, b, (((1,), (1,)), ((), ())), preferred_element_type=F32)


def _split(x):
    hi = x.astype(BF16)
    lo = (x - hi.astype(F32)).astype(BF16)
    return hi, lo


def _cparams(ndims, vmem_mb):
    return pltpu.CompilerParams(dimension_semantics=("arbitrary",) * ndims,
                                vmem_limit_bytes=vmem_mb << 20)


def _sigmoid(x):
    return 1.0 / (1.0 + jnp.exp(-x))


def _gelu_tanh(x):
    return 0.5 * x * (1.0 + jnp.tanh(0.7978845608028654 * (x + 0.044715 * (x * x * x))))


def _rms(x):
    return x * lax.rsqrt(jnp.mean(x * x, axis=-1, keepdims=True) + EPS)


def _seg_indicator(n):
    r = lax.broadcasted_iota(jnp.int32, (n, n), 0) // HEAD_DIM
    c = lax.broadcasted_iota(jnp.int32, (n, n), 1) // HEAD_DIM
    return jnp.where(r == c, 1.0, 0.0).astype(BF16)


def _head_rms(x, ind):
    hi, lo = _split(x * x)
    ss = _dot(hi, ind) + _dot(lo, ind)
    return x * lax.rsqrt(ss * (1.0 / HEAD_DIM) + EPS)


def _ada_kernel(c_ref, w_ref, b_ref, o_ref):
    c = c_ref[...]
    s = c * _sigmoid(c)
    s_hi, s_lo = _split(s)
    w_hi, w_lo = _split(w_ref[0])
    o_ref[0] = _dot(s_hi, w_hi) + _dot(s_lo, w_hi) + _dot(s_hi, w_lo) + b_ref[0]


def _ada_call(cs, w_ada, b_ada):
    depth, d, d6 = w_ada.shape
    r = cs.shape[0]
    tn = 1536
    return pl.pallas_call(
        _ada_kernel,
        grid=(depth, d6 // tn),
        in_specs=[pl.BlockSpec((r, d), lambda l, j: (0, 0)),
                  pl.BlockSpec((1, d, tn), lambda l, j: (l, 0, j)),
                  pl.BlockSpec((1, 1, tn), lambda l, j: (l, 0, j))],
        out_specs=pl.BlockSpec((1, r, tn), lambda l, j: (l, 0, j)),
        out_shape=jax.ShapeDtypeStruct((depth, r, d6), F32),
        compiler_params=_cparams(2, 40),
        name="ada",
    )(cs, w_ada, b_ada.reshape(depth, 1, d6))


def _proj_kernel(x_ref, mod_ref, w_ref, b_ref, qn_ref, kn_ref, gn_ref, ws_ref, gb_ref, cos_ref, sin_ref,
                 mq_ref, mk_ref, mv_ref, mo_ref, g_ref, gm_ref, aq_ref, ak_ref, av_ref, *, tm):
    h = (_rms(x_ref[...]) * (1.0 + mod_ref[0, 1:2, :]) + mod_ref[0, 0:1, :]).astype(BF16)

    def seg(a, b):
        return _dot(h, w_ref[:, a:b]) + b_ref[:, a:b]

    mq_ref[...] = seg(C_MQ, C_MK).astype(BF16)
    mk_ref[...] = (seg(C_MK, C_MV) * HEAD_DIM ** -0.5).astype(BF16)
    mv_ref[...] = seg(C_MV, C_MO).astype(BF16)
    mo_ref[...] = seg(C_MO, C_GU)

    g = seg(C_G, C_END)
    ls = jnp.minimum(g, 0.0) - jnp.log1p(jnp.exp(-jnp.abs(g)))
    r = lax.broadcasted_iota(jnp.int32, (tm, tm), 0)
    c = lax.broadcasted_iota(jnp.int32, (tm, tm), 1)
    same = (r // MLSTM_CHUNK) == (c // MLSTM_CHUNK)
    pre = jnp.where(same, jnp.where(c <= r, 1.0, 0.0), 0.0).astype(BF16)
    suf = jnp.where(same, jnp.where(c >= r, 1.0, 0.0), 0.0).astype(BF16)
    ls_hi, ls_lo = _split(ls)
    psum = _dot(pre, ls_hi) + _dot(pre, ls_lo)
    ssum = _dot(suf, ls_hi) + _dot(suf, ls_lo)
    lane = lax.broadcasted_iota(jnp.int32, (tm, LANES), 1)
    q4 = lane // MLSTM_HEADS
    g_ref[...] = jnp.where(q4 == 1, psum, jnp.where(q4 == 3, ssum, g))

    u = _gelu_tanh(seg(C_GU, C_GV))
    v = (_rms(_gelu_tanh(seg(C_GV, C_AQ))) * gn_ref[...]).astype(BF16)
    grp = lax.broadcasted_iota(jnp.int32, (GMLP_CHUNK, 2 * LANES), 1) // HEAD_DIM
    for ci in range(tm // GMLP_CHUNK):
        rows = slice(ci * GMLP_CHUNK, (ci + 1) * GMLP_CHUNK)
        full = _dot(ws_ref[...], v[rows, :])
        mixed = gb_ref[...]
        for gi in range(GMLP_GROUPS):
            mixed = mixed + jnp.where(grp == gi, full[gi * GMLP_CHUNK:(gi + 1) * GMLP_CHUNK, :], 0.0)
        gm_ref[rows, :] = u[rows, :] * mixed

    cos = cos_ref[...]
    sin = sin_ref[...]
    first_half = (lane % HEAD_DIM) < (HEAD_DIM // 2)
    lo_half = lane < HEAD_DIM

    def rope(blk):
        partner = jnp.where(first_half, pltpu.roll(blk, LANES - HEAD_DIM // 2, 1),
                            pltpu.roll(blk, HEAD_DIM // 2, 1))
        return blk * cos + partner * sin

    ind256 = _seg_indicator(2 * LANES)
    for s in range(2):
        qn = _head_rms(seg(C_AQ + s * 256, C_AQ + (s + 1) * 256), ind256) * qn_ref[:, s * 256:(s + 1) * 256]
        for j in range(2):
            blk = rope(qn[:, j * LANES:(j + 1) * LANES]) * HEAD_DIM ** -0.5
            aq_ref[:, s * 256 + j * LANES: s * 256 + (j + 1) * LANES] = blk.astype(BF16)

    kr = rope(_head_rms(seg(C_AK, C_AV), ind256[:LANES, :LANES]) * kn_ref[...])
    kroll = pltpu.roll(kr, HEAD_DIM, 1)
    ak_ref[:, 0:LANES] = jnp.where(lo_half, kr, kroll).astype(BF16)
    ak_ref[:, LANES:2 * LANES] = jnp.where(lo_half, kroll, kr).astype(BF16)

    vf = seg(C_AV, C_G)
    vroll = pltpu.roll(vf, HEAD_DIM, 1)
    av_ref[:, 0:LANES] = jnp.where(lo_half, vf, 1.0).astype(BF16)
    av_ref[:, LANES:2 * LANES] = jnp.where(lo_half, 1.0, vroll).astype(BF16)
    av_ref[:, 2 * LANES:3 * LANES] = jnp.where(lo_half, vroll, 1.0).astype(BF16)
    av_ref[:, 3 * LANES:4 * LANES] = jnp.where(lo_half, 1.0, vf).astype(BF16)


def _proj_call(x, mods, w, b, qn, kn, gn, ws4, gb, cos_t, sin_t, *, l, tm, nct, tpb):
    n, d = x.shape

    def mod_idx(i):
        return (l, jnp.where(i < nct, 0, 1 + (i - nct) // tpb), 0, 0)

    def rope_idx(i):
        return (jnp.where(i < nct, 0, 1 + (i - nct) % tpb), 0)

    row = lambda i: (i, 0)
    const2 = lambda i: (0, 0)
    lay = lambda i: (l, 0, 0)
    outs = [(256, BF16), (256, BF16), (256, BF16), (256, F32), (LANES, F32), (256, F32),
            (512, BF16), (256, BF16), (512, BF16)]
    return pl.pallas_call(
        functools.partial(_proj_kernel, tm=tm),
        grid=(n // tm,),
        in_specs=[pl.BlockSpec((tm, d), row),
                  pl.BlockSpec((None, 1, 6, d), mod_idx),
                  pl.BlockSpec((None, d, C_END), lay),
                  pl.BlockSpec((None, 1, C_END), lay),
                  pl.BlockSpec((None, 1, 512), lay),
                  pl.BlockSpec((None, 1, LANES), lay),
                  pl.BlockSpec((None, 1, 256), lay),
                  pl.BlockSpec((None, GMLP_GROUPS * GMLP_CHUNK, GMLP_CHUNK), lay),
                  pl.BlockSpec((None, GMLP_CHUNK, 256), lay),
                  pl.BlockSpec((tm, LANES), rope_idx),
                  pl.BlockSpec((tm, LANES), rope_idx)],
        out_specs=[pl.BlockSpec((tm, wd), row) for wd, _ in outs],
        out_shape=[jax.ShapeDtypeStruct((n, wd), dt) for wd, dt in outs],
        compiler_params=_cparams(1, 48),
        name="proj",
    )(x, mods, w, b, qn, kn, gn, ws4, gb, cos_t, sin_t)


def _mlstm_kernel(kf_ref, qf_ref, vf_ref, gf_ref, gtf_ref, kb_ref, qb_ref, vb_ref, gb_ref, gtb_ref,
                  hf_ref, hb_ref, c_ref, m_ref, *, nch):
    ch = MLSTM_CHUNK

    @pl.when(pl.program_id(1) == 0)
    def _():
        c_ref[...] = jnp.zeros_like(c_ref)
        m_ref[...] = jnp.zeros_like(m_ref)

    sub = lax.broadcasted_iota(jnp.int32, (ch, LANES), 0)
    lane = lax.broadcasted_iota(jnp.int32, (ch, LANES), 1)
    lo_half = lane < ch
    pos = lane % ch
    r2 = lax.broadcasted_iota(jnp.int32, (LANES, LANES), 0)
    c2 = lax.broadcasted_iota(jnp.int32, (LANES, LANES), 1)
    blockdiag = (r2 < ch) == (c2 < ch)
    ones_rows = jnp.ones((ch, LANES), BF16)
    dirs = ((kf_ref, qf_ref, vf_ref, gf_ref, gtf_ref, hf_ref), (kb_ref, qb_ref, vb_ref, gb_ref, gtb_ref, hb_ref))
    for d, (k_ref, q_ref, v_ref, g_ref, gt_ref, h_ref) in enumerate(dirs):
        mask = (sub <= pos) if d == 0 else (sub >= pos)
        sels = []
        for p in range(MLSTM_HEADS // 2):
            gi = d * 2 * MLSTM_HEADS + 2 * p + jnp.where(c2 >= ch, 1, 0)
            sels.append((jnp.where(r2 == gi, 1.0, 0.0) - jnp.where(r2 == gi + MLSTM_HEADS, 1.0, 0.0)).astype(BF16))
        order = range(nch) if d == 0 else range(nch - 1, -1, -1)
        for ci in order:
            rows = slice(ci * ch, (ci + 1) * ch)
            cl = slice(ci * LANES, (ci + 1) * LANES)
            g = g_ref[rows, :]
            g_hi = g.astype(BF16)
            g_r = g - g_hi.astype(F32)
            g_mid = g_r.astype(BF16)
            g_lo = (g_r - g_mid.astype(F32)).astype(BF16)
            hts = []
            for p in range(MLSTM_HEADS // 2):
                st = d * (MLSTM_HEADS // 2) + p
                kp = k_ref[rows, p * LANES:(p + 1) * LANES]
                zero = jnp.zeros_like(kp)
                qbd = jnp.where(blockdiag, q_ref[p * LANES:(p + 1) * LANES, cl], jnp.zeros((LANES, LANES), BF16))
                vat = jnp.concatenate([v_ref[p, :, cl], ones_rows], axis=0)
                irow = gt_ref[3 * st:3 * st + 1, cl]
                brow = gt_ref[3 * st + 1:3 * st + 2, cl]
                bend = gt_ref[3 * st + 2:3 * st + 3, cl]
                x = _dot(g_hi, sels[p]) + _dot(g_mid, sels[p]) + _dot(g_lo, sels[p])
                dmt = jnp.where(mask, brow + x, -jnp.inf)
                cmax = jnp.max(dmt, axis=0, keepdims=True)
                p0 = (_dot(kp, qbd) * jnp.exp(dmt - cmax)).astype(BF16)
                rt0 = _dot(vat, jnp.concatenate([jnp.where(lo_half, p0, zero), jnp.where(lo_half, zero, p0)], axis=0))
                dmax = jnp.max(bend + x, axis=0, keepdims=True)
                vw = (vat.astype(F32) * jnp.exp(bend + irow - brow - dmax)).astype(BF16)
                u0 = _dot(vw, jnp.concatenate([jnp.where(lo_half, kp, zero), jnp.where(lo_half, zero, kp)], axis=0))
                m = m_ref[st][0:1, :]
                cst = c_ref[st]
                inter = brow + m
                mt = jnp.maximum(inter, cmax)
                rt = jnp.exp(cmax - mt) * rt0 + jnp.exp(inter - mt) * _dot(cst.astype(BF16), qbd)
                hts.append(rt[0:ch] / jnp.maximum(jnp.abs(rt[ch:2 * ch]), jnp.exp(-mt)))
                mnew = jnp.maximum(bend + m, dmax)
                c_ref[st] = jnp.exp(bend + m - mnew) * cst + jnp.exp(dmax - mnew) * u0
                m_ref[st] = jnp.broadcast_to(mnew, (8, LANES))
            ht = jnp.concatenate(hts, axis=0).T
            h_ref[rows, 0:LANES] = ht[0:ch]
            h_ref[rows, LANES:2 * LANES] = ht[ch:2 * ch]


def _mlstm_layouts(mq, mv, g):
    n = mq.shape[0]
    nchunk = n // MLSTM_CHUNK
    qt = mq.T.reshape(256, nchunk, MLSTM_CHUNK)
    qtd = jnp.concatenate([qt, qt], axis=-1).reshape(256, nchunk * LANES)
    vt2 = mv.T.reshape(2, 2, HEAD_DIM, nchunk, MLSTM_CHUNK).transpose(0, 2, 3, 1, 4).reshape(2, HEAD_DIM, nchunk * LANES)
    gc = g[:, :16].reshape(nchunk, MLSTM_CHUNK, 16)
    rows = []
    for d in range(2):
        for p in range(2):
            ci = d * 8 + 2 * p
            irow = jnp.concatenate([gc[:, :, ci], gc[:, :, ci + 1]], axis=-1)
            brow = jnp.concatenate([gc[:, :, ci + 4], gc[:, :, ci + 5]], axis=-1)
            edge = MLSTM_CHUNK - 1 if d == 0 else 0
            bend = jnp.concatenate([jnp.broadcast_to(gc[:, edge, ci + 4 + e][:, None], (nchunk, MLSTM_CHUNK))
                                    for e in range(2)], axis=-1)
            rows += [irow, brow, bend]
    rows += [jnp.zeros_like(rows[0])] * 4
    return qtd, vt2, jnp.stack(rows, axis=0).reshape(16, nchunk * LANES)


def _mlstm_call(mk, qtd, vt2, g, gt3, *, b, rs, nsc, nsl, ncb):
    n = mk.shape[0]
    nch = rs // MLSTM_CHUNK

    def fwd(bi, j):
        return jnp.where(j < nsc, bi * nsc + j, ncb + bi * nsl + (j - nsc))

    def bwd(bi, j):
        return jnp.where(j < nsc, bi * nsc + (nsc - 1 - j), ncb + bi * nsl + (nsl - 1 - (j - nsc)))

    def specs(idx):
        return [pl.BlockSpec((rs, 256), lambda bi, j: (idx(bi, j), 0)),
                pl.BlockSpec((256, nch * LANES), lambda bi, j: (0, idx(bi, j))),
                pl.BlockSpec((2, HEAD_DIM, nch * LANES), lambda bi, j: (0, 0, idx(bi, j))),
                pl.BlockSpec((rs, LANES), lambda bi, j: (idx(bi, j), 0)),
                pl.BlockSpec((16, nch * LANES), lambda bi, j: (0, idx(bi, j)))]

    return pl.pallas_call(
        functools.partial(_mlstm_kernel, nch=nch),
        grid=(b, nsc + nsl),
        in_specs=specs(fwd) + specs(bwd),
        out_specs=[pl.BlockSpec((rs, 256), lambda bi, j: (fwd(bi, j), 0)),
                   pl.BlockSpec((rs, 256), lambda bi, j: (bwd(bi, j), 0))],
        out_shape=[jax.ShapeDtypeStruct((n, 256), F32)] * 2,
        scratch_shapes=[pltpu.VMEM((MLSTM_HEADS, LANES, LANES), F32),
                        pltpu.VMEM((MLSTM_HEADS, 8, LANES), F32)],
        compiler_params=_cparams(2, 32),
        name="mlstm",
    )(mk, qtd, vt2, g, gt3, mk, qtd, vt2, g, gt3)


def _attn_kernel(q_ref, k_ref, v_ref, o_ref, qs_ref, m_ref, aa_ref, ab_ref, *, tq, tc, t, tk, nqc):
    lane = lax.broadcasted_iota(jnp.int32, (tq, LANES), 1)
    lo_half = lane < HEAD_DIM
    for g in range(ATTN_KV_HEADS):
        for pp in range(2):
            qp = q_ref[:, (2 * g + pp) * LANES:(2 * g + pp + 1) * LANES]
            zero = jnp.zeros_like(qp)
            qs_ref[g, pp * tq:(pp + 1) * tq, :] = jnp.where(lo_half, qp, zero)
            qs_ref[g, (2 + pp) * tq:(3 + pp) * tq, :] = jnp.where(lo_half, zero, qp)
    m_ref[...] = jnp.full_like(m_ref, NEG_BIG)
    aa_ref[...] = jnp.zeros_like(aa_ref)
    ab_ref[...] = jnp.zeros_like(ab_ref)

    def chunk(start, size):
        for g in range(ATTN_KV_HEADS):
            kc = k_ref[0, pl.ds(start, size), g * LANES:(g + 1) * LANES]
            s = _dot_nt(qs_ref[g], kc)
            m_old = m_ref[g]
            m_new = jnp.maximum(m_old, jnp.max(s, axis=-1, keepdims=True))
            alpha = jnp.exp(m_old - m_new)
            p = jnp.exp(s - jnp.concatenate([m_new] * (size // LANES), axis=1)).astype(BF16)
            va = v_ref[0, pl.ds(start, size), (2 * g) * LANES:(2 * g + 1) * LANES]
            vb = v_ref[0, pl.ds(start, size), (2 * g + 1) * LANES:(2 * g + 2) * LANES]
            aa_ref[g] = alpha[:2 * tq] * aa_ref[g] + _dot(p[:2 * tq], va)
            ab_ref[g] = alpha[2 * tq:] * ab_ref[g] + _dot(p[2 * tq:], vb)
            m_ref[g] = m_new

    for c0 in range(0, tc, tk):
        chunk(c0, min(tk, tc - c0))

    @pl.when(pl.program_id(1) >= nqc)
    def _():
        def body(i, carry):
            chunk(pl.multiple_of(tc + i * tk, LANES), tk)
            return carry
        lax.fori_loop(0, t // tk, body, 0, unroll=2)

    for g in range(ATTN_KV_HEADS):
        for pp in range(2):
            a = aa_ref[g, pp * tq:(pp + 1) * tq, :]
            bq = ab_ref[g, pp * tq:(pp + 1) * tq, :]
            oa = a * pltpu.roll(1.0 / a, HEAD_DIM, 1)
            ob = bq * pltpu.roll(1.0 / bq, HEAD_DIM, 1)
            o_ref[:, (2 * g + pp) * LANES:(2 * g + pp + 1) * LANES] = jnp.where(lo_half, oa, ob)


def _attn_call(aq, kd, vd, *, b, tq, tc, t, tk, ncq):
    n = aq.shape[0]
    nqc, nql = tc // tq, t // tq

    def qidx(bi, j):
        return (jnp.where(j < nqc, bi * nqc + j, ncq + bi * nql + (j - nqc)), 0)

    return pl.pallas_call(
        functools.partial(_attn_kernel, tq=tq, tc=tc, t=t, tk=tk, nqc=nqc),
        grid=(b, nqc + nql),
        in_specs=[pl.BlockSpec((tq, 512), qidx),
                  pl.BlockSpec((1, tc + t, 256), lambda bi, j: (bi, 0, 0)),
                  pl.BlockSpec((1, tc + t, 512), lambda bi, j: (bi, 0, 0))],
        out_specs=pl.BlockSpec((tq, 512), qidx),
        out_shape=jax.ShapeDtypeStruct((n, 512), F32),
        scratch_shapes=[pltpu.VMEM((ATTN_KV_HEADS, 4 * tq, LANES), BF16),
                        pltpu.VMEM((ATTN_KV_HEADS, 4 * tq, LANES), F32),
                        pltpu.VMEM((ATTN_KV_HEADS, 2 * tq, LANES), F32),
                        pltpu.VMEM((ATTN_KV_HEADS, 2 * tq, LANES), F32)],
        compiler_params=_cparams(2, 48),
        name="attn",
    )(aq, kd, vd)


def _merge_kernel(hf_ref, hb_ref, mo_ref, gm_ref, a_ref, x_ref, mod_ref, mn_ref, wo_ref, rwh_ref, rwl_ref, rb_ref,
                  xo_ref, h2_ref, lg_ref):
    ind = _seg_indicator(2 * LANES)
    parts = [_sigmoid(mo_ref[...]) * (hf_ref[...] + hb_ref[...]), gm_ref[...],
             a_ref[:, 0:256], a_ref[:, 256:512]]
    y = jnp.concatenate([_head_rms(pt, ind) for pt in parts], axis=1) * mn_ref[...]
    xn = x_ref[...] + mod_ref[0, 2:3, :] * _dot(y.astype(BF16), wo_ref[...])
    xo_ref[...] = xn
    h2 = _rms(xn) * (1.0 + mod_ref[0, 4:5, :]) + mod_ref[0, 3:4, :]
    for sl in range(h2_ref.shape[1]):
        h2_ref[:, sl, :] = h2[:, sl * LANES:(sl + 1) * LANES]
    hi, lo = _split(h2)
    lg_ref[...] = _dot(hi, rwh_ref[...]) + _dot(lo, rwh_ref[...]) + _dot(hi, rwl_ref[...]) + rb_ref[...]


def _merge_call(hf, hb, mo, gm, a, x, mods, mn, wo, rwh, rwl, rb, *, l, tm, nct, tpb):
    n, d = x.shape

    def mod_idx(i):
        return (l, jnp.where(i < nct, 0, 1 + (i - nct) // tpb), 0, 0)

    row = lambda i: (i, 0)
    lay = lambda i: (l, 0, 0)
    return pl.pallas_call(
        _merge_kernel,
        grid=(n // tm,),
        in_specs=[pl.BlockSpec((tm, 256), row), pl.BlockSpec((tm, 256), row), pl.BlockSpec((tm, 256), row),
                  pl.BlockSpec((tm, 256), row), pl.BlockSpec((tm, 512), row), pl.BlockSpec((tm, d), row),
                  pl.BlockSpec((None, 1, 6, d), mod_idx),
                  pl.BlockSpec((None, 1, d), lay),
                  pl.BlockSpec((None, d, d), lay),
                  pl.BlockSpec((None, d, LANES), lay),
                  pl.BlockSpec((None, d, LANES), lay),
                  pl.BlockSpec((None, 1, LANES), lay)],
        out_specs=[pl.BlockSpec((tm, d), row), pl.BlockSpec((tm, d // LANES, LANES), lambda i: (i, 0, 0)),
                   pl.BlockSpec((tm, LANES), row)],
        out_shape=[jax.ShapeDtypeStruct((n, d), F32), jax.ShapeDtypeStruct((n, d // LANES, LANES), F32),
                   jax.ShapeDtypeStruct((n, LANES), F32)],
        compiler_params=_cparams(1, 48),
        name="merge",
    )(hf, hb, mo, gm, a, x, mods, mn, wo, rwh, rwl, rb)


def _wprep_kernel(w_ref, o_ref, *, f):
    slab = 2 * LANES
    r = lax.broadcasted_iota(jnp.int32, (slab, slab), 0)
    c = lax.broadcasted_iota(jnp.int32, (slab, slab), 1)
    perm = jnp.where(r == jnp.where(c < LANES, 2 * c, 2 * (c - LANES) + 1), 1.0, 0.0).astype(BF16)
    for s in range(2 * f // slab):
        y = _dot(w_ref[:, s * slab:(s + 1) * slab].astype(BF16), perm).astype(BF16)
        o_ref[:, s * LANES:(s + 1) * LANES] = y[:, :LANES]
        o_ref[:, f + s * LANES:f + (s + 1) * LANES] = y[:, LANES:]


def _wprep_call(w_gate_up):
    depth, ne, d, f2 = w_gate_up.shape
    rows = 512
    w2 = w_gate_up.reshape(depth * ne * d, f2)
    out = pl.pallas_call(
        functools.partial(_wprep_kernel, f=f2 // 2),
        grid=(depth * ne * d // rows,),
        in_specs=[pl.BlockSpec((rows, f2), lambda i: (i, 0))],
        out_specs=pl.BlockSpec((rows, f2), lambda i: (i, 0)),
        out_shape=jax.ShapeDtypeStruct((depth * ne * d, f2), BF16),
        compiler_params=_cparams(1, 32),
        name="wprep",
    )(w2)
    return out.reshape(depth, ne, d, f2)


def _moe_kernel(be_ref, nv_ref, tok0_ref, tokn_ref, h_hbm, wgl_ref, wd_ref, bgl_ref, bd_ref, y_ref,
                xbuf, act_ref, sem, *, tmb, f):
    i = pl.program_id(0)
    nv = nv_ref[0]
    slot = i % 2
    cw = 2 * LANES
    nchunk = f // cw
    per = tmb // (2 * nchunk)

    def request(tok_ref, r, dst):
        pltpu.make_async_copy(h_hbm.at[tok_ref[0, 0, r]], xbuf.at[dst, r], sem.at[dst]).start()

    @pl.when(i == 0)
    def _():
        def issue(r, carry):
            request(tok0_ref, r, 0)
            return carry
        lax.fori_loop(0, tmb, issue, 0, unroll=8)

    @pl.when(i <= nv)
    def _():
        pltpu.make_async_copy(h_hbm.at[pl.ds(0, tmb)], xbuf.at[slot], sem.at[slot]).wait()

    def compute(cur):
        xb = jnp.concatenate([xbuf[cur, :, sl, :] for sl in range(xbuf.shape[2])], axis=1).astype(BF16)
        for j in range(nchunk):
            cols = slice(j * cw, (j + 1) * cw)
            ucols = slice(f + j * cw, f + (j + 1) * cw)
            x_glu = jnp.minimum(_dot(xb, wgl_ref[:, cols]) + bgl_ref[:, cols], SWIGLU_LIMIT)
            for r in range(2 * j * per, (2 * j + 1) * per):
                request(tokn_ref, r, 1 - cur)
            x_lin = jnp.clip(_dot(xb, wgl_ref[:, ucols]) + bgl_ref[:, ucols], -SWIGLU_LIMIT, SWIGLU_LIMIT)
            for r in range((2 * j + 1) * per, (2 * j + 2) * per):
                request(tokn_ref, r, 1 - cur)
            act_ref[:, cols] = (x_glu * _sigmoid(SWIGLU_ALPHA * x_glu) * (x_lin + 1.0)).astype(BF16)
        act = act_ref[...]
        for j in range(y_ref.shape[1] // cw):
            cols = slice(j * cw, (j + 1) * cw)
            y_ref[:, cols] = _dot(act, wd_ref[:, cols]) + bd_ref[:, cols]

    for cur in range(2):
        pl.when(jnp.logical_and(i < nv, slot == cur))(functools.partial(compute, cur))

    @pl.when(i >= nv)
    def _():
        y_ref[...] = jnp.zeros_like(y_ref)


def _moe_call(block_expert, nvalid, slot_tok, h2, wgl, wd, bgl, bd, *, l, tmb):
    d = wd.shape[-1]
    nb = block_expert.shape[0]
    f = wd.shape[-2]
    wmap = lambda i, be, nv: (l, be[i], 0, 0)
    grid_spec = pltpu.PrefetchScalarGridSpec(
        num_scalar_prefetch=2,
        grid=(nb,),
        in_specs=[pl.BlockSpec((1, 1, tmb), lambda i, be, nv: (0, 0, 0), memory_space=pltpu.SMEM),
                  pl.BlockSpec((1, 1, tmb), lambda i, be, nv: (jnp.minimum(i + 1, nb - 1), 0, 0),
                               memory_space=pltpu.SMEM),
                  pl.BlockSpec(memory_space=pl.ANY),
                  pl.BlockSpec((None, None, d, 2 * f), wmap),
                  pl.BlockSpec((None, None, f, d), wmap),
                  pl.BlockSpec((None, None, 1, 2 * f), wmap),
                  pl.BlockSpec((None, None, 1, d), wmap)],
        out_specs=pl.BlockSpec((tmb, d), lambda i, be, nv: (i, 0)),
        scratch_shapes=[pltpu.VMEM((2, tmb, d // LANES, LANES), F32), pltpu.VMEM((tmb, f), BF16),
                        pltpu.SemaphoreType.DMA((2,))],
    )
    return pl.pallas_call(
        functools.partial(_moe_kernel, tmb=tmb, f=f),
        grid_spec=grid_spec,
        out_shape=jax.ShapeDtypeStruct((nb * tmb, d), F32),
        compiler_params=_cparams(1, 48),
        name="moe",
    )(block_expert, nvalid, slot_tok, slot_tok, h2, wgl, wd, bgl, bd)


def _combine_kernel(dest_ref, y_hbm, gate_ref, x_ref, mod_ref, o_ref, buf, sem, *, tt):
    def issue(r, carry):
        for k in range(TOP_K):
            src = dest_ref[0, 0, r * TOP_K + k]
            pltpu.make_async_copy(y_hbm.at[pl.ds(src, 1)], buf.at[k, pl.ds(r, 1)], sem).start()
        return carry
    lax.fori_loop(0, tt, issue, 0, unroll=4)
    def drain(r, carry):
        for k in range(TOP_K):
            src = dest_ref[0, 0, r * TOP_K + k]
            pltpu.make_async_copy(y_hbm.at[pl.ds(src, 1)], buf.at[k, pl.ds(r, 1)], sem).wait()
        return carry
    lax.fori_loop(0, tt, drain, 0, unroll=4)
    gate = gate_ref[...]
    f = gate[:, 0:1] * buf[0]
    for k in range(1, TOP_K):
        f = f + gate[:, k:k + 1] * buf[k]
    o_ref[...] = x_ref[...] + mod_ref[0, 5:6, :] * f


def _combine_call(dest, y_slots, gates, x, mods, *, l, tt, nct, tpb):
    n, d = x.shape

    def mod_idx(i):
        return (l, jnp.where(i < nct, 0, 1 + (i - nct) // tpb), 0, 0)

    row = lambda i: (i, 0)
    return pl.pallas_call(
        functools.partial(_combine_kernel, tt=tt),
        grid=(n // tt,),
        in_specs=[pl.BlockSpec((1, 1, tt * TOP_K), lambda i: (i, 0, 0), memory_space=pltpu.SMEM),
                  pl.BlockSpec(memory_space=pl.ANY),
                  pl.BlockSpec((tt, LANES), row),
                  pl.BlockSpec((tt, d), row),
                  pl.BlockSpec((None, 1, 6, d), mod_idx)],
        out_specs=pl.BlockSpec((tt, d), row),
        out_shape=jax.ShapeDtypeStruct((n, d), F32),
        scratch_shapes=[pltpu.VMEM((TOP_K, tt, d), F32), pltpu.SemaphoreType.DMA(())],
        compiler_params=_cparams(1, 32),
        name="combine",
    )(dest, y_slots, gates, x, mods)


def _final_kernel(x_ref, g_ref, o_ref):
    o_ref[...] = _rms(x_ref[...]) * g_ref[...]


def _final_call(x, gain, *, tm, nct, nl):
    d = x.shape[1]
    return pl.pallas_call(
        _final_kernel,
        grid=(nl // tm,),
        in_specs=[pl.BlockSpec((tm, d), lambda i: (nct + i, 0)), pl.BlockSpec((1, d), lambda i: (0, 0))],
        out_specs=pl.BlockSpec((tm, d), lambda i: (i, 0)),
        out_shape=jax.ShapeDtypeStruct((nl, d), F32),
        compiler_params=_cparams(1, 32),
        name="final_norm",
    )(x, gain.reshape(1, d))


def _route(logits, tmb, n_blocks):
    n = logits.shape[0]
    top_val, top_idx = lax.top_k(logits[:, :N_EXPERTS], TOP_K)
    gates = jax.nn.softmax(top_val, axis=-1)
    onehot = jnp.sum(jax.nn.one_hot(top_idx, N_EXPERTS, dtype=jnp.int32), axis=1)
    before = jnp.cumsum(onehot, axis=0) - onehot
    counts = before[-1] + onehot[-1]
    padded = (counts + tmb - 1) // tmb * tmb
    ends = jnp.cumsum(padded)
    dest = (ends - padded)[top_idx] + jnp.take_along_axis(before, top_idx, axis=1)
    nvalid = (ends[-1] // tmb).astype(jnp.int32)
    blk = jnp.minimum(jnp.arange(n_blocks, dtype=jnp.int32), nvalid - 1)
    block_expert = jnp.minimum(jnp.searchsorted(ends, blk * tmb, side='right'), N_EXPERTS - 1).astype(jnp.int32)
    na = n * TOP_K
    order = jnp.sort(top_idx.reshape(-1).astype(jnp.int32) * na + jnp.arange(na, dtype=jnp.int32)) % na
    slot = jnp.arange(n_blocks * tmb, dtype=jnp.int32)
    se = block_expert[slot // tmb]
    rank = slot - (ends - padded)[se]
    src = jnp.clip((jnp.cumsum(counts) - counts)[se] + rank, 0, na - 1)
    slot_tok = jnp.where((rank < counts[se]) & (slot // tmb < nvalid), order[src] // TOP_K, 0).astype(jnp.int32)
    return gates, dest.astype(jnp.int32), slot_tok, block_expert, nvalid.reshape(1)


def _rope_tables(t, tm):
    rows = t // GRID_W
    row = jnp.repeat(jnp.arange(rows, dtype=F32), GRID_W)
    col = jnp.tile(jnp.arange(GRID_W, dtype=F32), rows)
    npair = HEAD_DIM // 4
    inv_freq = ROPE_THETA ** (-jnp.arange(npair, dtype=F32) / npair)
    ang = jnp.concatenate([row[:, None] * inv_freq, col[:, None] * inv_freq], axis=-1)
    cos = jnp.tile(jnp.cos(ang), (1, 4))
    sin = jnp.tile(jnp.concatenate([-jnp.sin(ang), jnp.sin(ang)], axis=-1), (1, 2))
    cos = jnp.concatenate([jnp.ones((tm, LANES), F32), cos], axis=0)
    sin = jnp.concatenate([jnp.zeros((tm, LANES), F32), sin], axis=0)
    return cos, sin


def _tile(nc, t, cap):
    for cand in (1024, 512, 256, 128):
        if cand <= cap and nc % cand == 0 and t % cand == 0:
            return cand
    raise ValueError("unsupported sequence lengths")


def kernel(x, c, ctx, c_ctx, w_ada, b_ada, w_in, b_in, q_norm, k_norm, gmlp_norm, gmlp_ws, gmlp_b, mix_norm, w_out,
           router_w, router_b, w_gate_up, b_gate_up, w_down, b_down, final_norm):
    b, t, d = x.shape
    tc = ctx.shape[1]
    depth = w_in.shape[0]
    nc, nl = b * tc, b * t
    n = nc + nl
    tm = _tile(nc, t, 512)
    rs = _tile(tc, t, 256)
    tq = _tile(tc, t, 256)
    tk = _tile(t, t, 512)
    tmb = 512
    tt = _tile(nc, t, 256)
    nct, tpb = nc // tm, t // tm
    n_blocks = -(-n * TOP_K // tmb) + N_EXPERTS

    def heads_0213(a, axis):
        blk = [lax.slice_in_dim(a, i * HEAD_DIM, (i + 1) * HEAD_DIM, axis=axis) for i in (0, 2, 1, 3)]
        rest = lax.slice_in_dim(a, MLSTM_HEADS * HEAD_DIM, a.shape[axis], axis=axis)
        return jnp.concatenate(blk + [rest], axis=axis)

    def cols(a):
        pad = jnp.zeros(a.shape[:-1] + (LANES - 16,), a.dtype)
        return jnp.concatenate([a[..., 0:768], heads_0213(a[..., 768:1024], a.ndim - 1), a[..., 1040:2320],
                                a[..., 1024:1040], pad], axis=-1)

    w_in_r = cols(w_in).astype(BF16)
    b_in_r = cols(b_in).reshape(depth, 1, C_END)
    qn = jnp.tile(q_norm, (1, ATTN_HEADS)).reshape(depth, 1, 512)
    kn = jnp.tile(k_norm, (1, 2)).reshape(depth, 1, LANES)
    gn = gmlp_norm.reshape(depth, 1, 256)
    ws4 = gmlp_ws.reshape(depth, GMLP_GROUPS * GMLP_CHUNK, GMLP_CHUNK).astype(BF16)
    gb = jnp.repeat(jnp.swapaxes(gmlp_b, 1, 2), HEAD_DIM, axis=2)
    mn = heads_0213(mix_norm, 1).reshape(depth, 1, d)
    wo = heads_0213(w_out, 1).astype(BF16)
    rw = jnp.pad(router_w, ((0, 0), (0, 0), (0, LANES - N_EXPERTS)))
    rwh = rw.astype(BF16)
    rwl = (rw - rwh.astype(F32)).astype(BF16)
    rb = jnp.pad(router_b, ((0, 0), (0, LANES - N_EXPERTS))).reshape(depth, 1, LANES)
    wgl = _wprep_call(w_gate_up)
    wd = w_down.astype(BF16)
    ne, ff = w_down.shape[1], w_down.shape[2]
    bgl = jnp.concatenate([b_gate_up[..., 0::2], b_gate_up[..., 1::2]], axis=-1).reshape(depth, ne, 1, 2 * ff)
    bd = b_down.reshape(depth, ne, 1, d)
    cos_t, sin_t = _rope_tables(t, tm)

    r = -(-(b + 1) // 8) * 8
    cs = jnp.zeros((r, d), F32).at[0].set(c_ctx).at[1:b + 1].set(c)
    mods = _ada_call(cs, w_ada, b_ada).reshape(depth, r, 6, d)

    xs = jnp.concatenate([ctx.reshape(nc, d), x.reshape(nl, d)], axis=0)
    for l in range(depth):
        mq, mk, mv, mo, g, gm, aq, akd, avd = _proj_call(
            xs, mods, w_in_r, b_in_r, qn, kn, gn, ws4, gb, cos_t, sin_t, l=l, tm=tm, nct=nct, tpb=tpb)
        qtd, vt2, gt3 = _mlstm_layouts(mq, mv, g)
        hf, hb = _mlstm_call(mk, qtd, vt2, g, gt3, b=b, rs=rs, nsc=tc // rs, nsl=t // rs, ncb=nc // rs)
        kd = jnp.concatenate([akd[:nc].reshape(b, tc, 256), akd[nc:].reshape(b, t, 256)], axis=1)
        vd = jnp.concatenate([avd[:nc].reshape(b, tc, 512), avd[nc:].reshape(b, t, 512)], axis=1)
        a = _attn_call(aq, kd, vd, b=b, tq=tq, tc=tc, t=t, tk=tk, ncq=nc // tq)
        xs, h2, logits = _merge_call(hf, hb, mo, gm, a, xs, mods, mn, wo, rwh, rwl, rb, l=l, tm=tm, nct=nct, tpb=tpb)
        gates, dest, slot_tok, block_expert, nvalid = _route(logits, tmb, n_blocks)
        y_slots = _moe_call(block_expert, nvalid, slot_tok.reshape(n_blocks, 1, tmb), h2, wgl, wd, bgl, bd,
                            l=l, tmb=tmb)
        xs = _combine_call(dest.reshape(n // tt, 1, tt * TOP_K), y_slots,
                           jnp.pad(gates, ((0, 0), (0, LANES - TOP_K))), xs, mods, l=l, tt=tt,
                           nct=nc // tt, tpb=t // tt)
    out = _final_call(xs, final_norm, tm=tm, nct=nct, nl=nl)
    return out.reshape(b, t, d)
```

```python
import functools

import jax
import jax.numpy as jnp
from jax import lax
from jax.experimental import pallas as pl
from jax.experimental.pallas import tpu as pltpu

F32 = jnp.float32
BF16 = jnp.bfloat16

HEAD_DIM = 64
EPS = 1e-6
GRID_W = 64
ROPE_THETA = 10000.0
MLSTM_HEADS = 4
MLSTM_CHUNK = 64
GMLP_GROUPS = 4
GMLP_CHUNK = 128
ATTN_HEADS = 8
ATTN_KV_HEADS = 2
N_EXPERTS = 32
TOP_K = 4
SWIGLU_LIMIT = 7.0
SWIGLU_ALPHA = 1.702
LANES = 128
NEG_BIG = -1e30

C_MQ, C_MK, C_MV, C_MO, C_GU, C_GV, C_AQ, C_AK, C_AV, C_G, C_END = (
    0, 256, 512, 768, 1024, 1280, 1536, 2048, 2176, 2304, 2432)


def _dot(a, b):
    return jnp.dot(a, b, preferred_element_type=F32)


def _dot_nt(a, b):
    return lax.dot_general(a, b, (((1,), (1,)), ((), ())), preferred_element_type=F32)


def _split(x):
    hi = x.astype(BF16)
    lo = (x - hi.astype(F32)).astype(BF16)
    return hi, lo


def _cparams(ndims, vmem_mb):
    return pltpu.CompilerParams(dimension_semantics=("arbitrary",) * ndims,
                                vmem_limit_bytes=vmem_mb << 20)


def _sigmoid(x):
    return 1.0 / (1.0 + jnp.exp(-x))


def _gelu_tanh(x):
    return 0.5 * x * (1.0 + jnp.tanh(0.7978845608028654 * (x + 0.044715 * (x * x * x))))


def _rms(x):
    return x * lax.rsqrt(jnp.mean(x * x, axis=-1, keepdims=True) + EPS)


def _seg_indicator(n):
    r = lax.broadcasted_iota(jnp.int32, (n, n), 0) // HEAD_DIM
    c = lax.broadcasted_iota(jnp.int32, (n, n), 1) // HEAD_DIM
    return jnp.where(r == c, 1.0, 0.0).astype(BF16)


def _head_rms(x, ind):
    hi, lo = _split(x * x)
    ss = _dot(hi, ind) + _dot(lo, ind)
    return x * lax.rsqrt(ss * (1.0 / HEAD_DIM) + EPS)


def _ada_kernel(c_ref, w_ref, b_ref, o_ref):
    c = c_ref[...]
    s = c * _sigmoid(c)
    s_hi, s_lo = _split(s)
    w_hi, w_lo = _split(w_ref[0])
    o_ref[0] = _dot(s_hi, w_hi) + _dot(s_lo, w_hi) + _dot(s_hi, w_lo) + b_ref[0]


def _ada_call(cs, w_ada, b_ada):
    depth, d, d6 = w_ada.shape
    r = cs.shape[0]
    tn = 1536
    return pl.pallas_call(
        _ada_kernel,
        grid=(depth, d6 // tn),
        in_specs=[pl.BlockSpec((r, d), lambda l, j: (0, 0)),
                  pl.BlockSpec((1, d, tn), lambda l, j: (l, 0, j)),
                  pl.BlockSpec((1, 1, tn), lambda l, j: (l, 0, j))],
        out_specs=pl.BlockSpec((1, r, tn), lambda l, j: (l, 0, j)),
        out_shape=jax.ShapeDtypeStruct((depth, r, d6), F32),
        compiler_params=_cparams(2, 40),
        name="ada",
    )(cs, w_ada, b_ada.reshape(depth, 1, d6))


def _proj_kernel(x_ref, mod_ref, w_ref, b_ref, qn_ref, kn_ref, gn_ref, ws_ref, gb_ref, cos_ref, sin_ref,
                 mq_ref, mk_ref, mv_ref, mo_ref, g_ref, gm_ref, aq_ref, ak_ref, av_ref, *, tm):
    h = (_rms(x_ref[...]) * (1.0 + mod_ref[0, 1:2, :]) + mod_ref[0, 0:1, :]).astype(BF16)

    def seg(a, b):
        return _dot(h, w_ref[:, a:b]) + b_ref[:, a:b]

    mq_ref[...] = seg(C_MQ, C_MK).astype(BF16)
    mk_ref[...] = (seg(C_MK, C_MV) * HEAD_DIM ** -0.5).astype(BF16)
    mv_ref[...] = seg(C_MV, C_MO).astype(BF16)
    mo_ref[...] = seg(C_MO, C_GU)

    g = seg(C_G, C_END)
    ls = jnp.minimum(g, 0.0) - jnp.log1p(jnp.exp(-jnp.abs(g)))
    r = lax.broadcasted_iota(jnp.int32, (tm, tm), 0)
    c = lax.broadcasted_iota(jnp.int32, (tm, tm), 1)
    same = (r // MLSTM_CHUNK) == (c // MLSTM_CHUNK)
    pre = jnp.where(same, jnp.where(c <= r, 1.0, 0.0), 0.0).astype(BF16)
    suf = jnp.where(same, jnp.where(c >= r, 1.0, 0.0), 0.0).astype(BF16)
    ls_hi, ls_lo = _split(ls)
    psum = _dot(pre, ls_hi) + _dot(pre, ls_lo)
    ssum = _dot(suf, ls_hi) + _dot(suf, ls_lo)
    lane = lax.broadcasted_iota(jnp.int32, (tm, LANES), 1)
    q4 = lane // MLSTM_HEADS
    g_ref[...] = jnp.where(q4 == 1, psum, jnp.where(q4 == 3, ssum, g))

    u = _gelu_tanh(seg(C_GU, C_GV))
    v = (_rms(_gelu_tanh(seg(C_GV, C_AQ))) * gn_ref[...]).astype(BF16)
    grp = lax.broadcasted_iota(jnp.int32, (GMLP_CHUNK, 2 * LANES), 1) // HEAD_DIM
    for ci in range(tm // GMLP_CHUNK):
        rows = slice(ci * GMLP_CHUNK, (ci + 1) * GMLP_CHUNK)
        full = _dot(ws_ref[...], v[rows, :])
        mixed = gb_ref[...]
        for gi in range(GMLP_GROUPS):
            mixed = mixed + jnp.where(grp == gi, full[gi * GMLP_CHUNK:(gi + 1) * GMLP_CHUNK, :], 0.0)
        gm_ref[rows, :] = u[rows, :] * mixed

    cos = cos_ref[...]
    sin = sin_ref[...]
    first_half = (lane % HEAD_DIM) < (HEAD_DIM // 2)
    lo_half = lane < HEAD_DIM

    def rope(blk):
        partner = jnp.where(first_half, pltpu.roll(blk, LANES - HEAD_DIM // 2, 1),
                            pltpu.roll(blk, HEAD_DIM // 2, 1))
        return blk * cos + partner * sin

    ind256 = _seg_indicator(2 * LANES)
    for s in range(2):
        qn = _head_rms(seg(C_AQ + s * 256, C_AQ + (s + 1) * 256), ind256) * qn_ref[:, s * 256:(s + 1) * 256]
        for j in range(2):
            blk = rope(qn[:, j * LANES:(j + 1) * LANES]) * HEAD_DIM ** -0.5
            aq_ref[:, s * 256 + j * LANES: s * 256 + (j + 1) * LANES] = blk.astype(BF16)

    kr = rope(_head_rms(seg(C_AK, C_AV), ind256[:LANES, :LANES]) * kn_ref[...])
    kroll = pltpu.roll(kr, HEAD_DIM, 1)
    ak_ref[:, 0:LANES] = jnp.where(lo_half, kr, kroll).astype(BF16)
    ak_ref[:, LANES:2 * LANES] = jnp.where(lo_half, kroll, kr).astype(BF16)

    vf = seg(C_AV, C_G)
    vroll = pltpu.roll(vf, HEAD_DIM, 1)
    av_ref[:, 0:LANES] = jnp.where(lo_half, vf, 1.0).astype(BF16)
    av_ref[:, LANES:2 * LANES] = jnp.where(lo_half, 1.0, vroll).astype(BF16)
    av_ref[:, 2 * LANES:3 * LANES] = jnp.where(lo_half, vroll, 1.0).astype(BF16)
    av_ref[:, 3 * LANES:4 * LANES] = jnp.where(lo_half, 1.0, vf).astype(BF16)


def _proj_call(x, mods, w, b, qn, kn, gn, ws4, gb, cos_t, sin_t, *, l, tm, nct, tpb):
    n, d = x.shape

    def mod_idx(i):
        return (l, jnp.where(i < nct, 0, 1 + (i - nct) // tpb), 0, 0)

    def rope_idx(i):
        return (jnp.where(i < nct, 0, 1 + (i - nct) % tpb), 0)

    row = lambda i: (i, 0)
    const2 = lambda i: (0, 0)
    lay = lambda i: (l, 0, 0)
    outs = [(256, BF16), (256, BF16), (256, BF16), (256, F32), (LANES, F32), (256, F32),
            (512, BF16), (256, BF16), (512, BF16)]
    return pl.pallas_call(
        functools.partial(_proj_kernel, tm=tm),
        grid=(n // tm,),
        in_specs=[pl.BlockSpec((tm, d), row),
                  pl.BlockSpec((None, 1, 6, d), mod_idx),
                  pl.BlockSpec((None, d, C_END), lay),
                  pl.BlockSpec((None, 1, C_END), lay),
                  pl.BlockSpec((None, 1, 512), lay),
                  pl.BlockSpec((None, 1, LANES), lay),
                  pl.BlockSpec((None, 1, 256), lay),
                  pl.BlockSpec((None, GMLP_GROUPS * GMLP_CHUNK, GMLP_CHUNK), lay),
                  pl.BlockSpec((None, GMLP_CHUNK, 256), lay),
                  pl.BlockSpec((tm, LANES), rope_idx),
                  pl.BlockSpec((tm, LANES), rope_idx)],
        out_specs=[pl.BlockSpec((tm, wd), row) for wd, _ in outs],
        out_shape=[jax.ShapeDtypeStruct((n, wd), dt) for wd, dt in outs],
        compiler_params=_cparams(1, 48),
        name="proj",
    )(x, mods, w, b, qn, kn, gn, ws4, gb, cos_t, sin_t)


def _mlstm_kernel(kf_ref, qf_ref, vf_ref, gf_ref, gtf_ref, kb_ref, qb_ref, vb_ref, gb_ref, gtb_ref,
                  hf_ref, hb_ref, c_ref, m_ref, *, nch):
    ch = MLSTM_CHUNK

    @pl.when(pl.program_id(1) == 0)
    def _():
        c_ref[...] = jnp.zeros_like(c_ref)
        m_ref[...] = jnp.zeros_like(m_ref)

    sub = lax.broadcasted_iota(jnp.int32, (ch, LANES), 0)
    lane = lax.broadcasted_iota(jnp.int32, (ch, LANES), 1)
    lo_half = lane < ch
    pos = lane % ch
    r2 = lax.broadcasted_iota(jnp.int32, (LANES, LANES), 0)
    c2 = lax.broadcasted_iota(jnp.int32, (LANES, LANES), 1)
    blockdiag = (r2 < ch) == (c2 < ch)
    ones_rows = jnp.ones((ch, LANES), BF16)
    dirs = ((kf_ref, qf_ref, vf_ref, gf_ref, gtf_ref, hf_ref), (kb_ref, qb_ref, vb_ref, gb_ref, gtb_ref, hb_ref))
    for d, (k_ref, q_ref, v_ref, g_ref, gt_ref, h_ref) in enumerate(dirs):
        mask = (sub <= pos) if d == 0 else (sub >= pos)
        sels = []
        for p in range(MLSTM_HEADS // 2):
            gi = d * 2 * MLSTM_HEADS + 2 * p + jnp.where(c2 >= ch, 1, 0)
            sels.append((jnp.where(r2 == gi, 1.0, 0.0) - jnp.where(r2 == gi + MLSTM_HEADS, 1.0, 0.0)).astype(BF16))
        order = range(nch) if d == 0 else range(nch - 1, -1, -1)
        for ci in order:
            rows = slice(ci * ch, (ci + 1) * ch)
            cl = slice(ci * LANES, (ci + 1) * LANES)
            g = g_ref[rows, :]
            g_hi = g.astype(BF16)
            g_r = g - g_hi.astype(F32)
            g_mid = g_r.astype(BF16)
            g_lo = (g_r - g_mid.astype(F32)).astype(BF16)
            hts = []
            for p in range(MLSTM_HEADS // 2):
                st = d * (MLSTM_HEADS // 2) + p
                kp = k_ref[rows, p * LANES:(p + 1) * LANES]
                zero = jnp.zeros_like(kp)
                qbd = jnp.where(blockdiag, q_ref[p * LANES:(p + 1) * LANES, cl], jnp.zeros((LANES, LANES), BF16))
                vat = jnp.concatenate([v_ref[p, :, cl], ones_rows], axis=0)
                irow = gt_ref[3 * st:3 * st + 1, cl]
                brow = gt_ref[3 * st + 1:3 * st + 2, cl]
                bend = gt_ref[3 * st + 2:3 * st + 3, cl]
                x = _dot(g_hi, sels[p]) + _dot(g_mid, sels[p]) + _dot(g_lo, sels[p])
                dmt = jnp.where(mask, brow + x, -jnp.inf)
                cmax = jnp.max(dmt, axis=0, keepdims=True)
                p0 = (_dot(kp, qbd) * jnp.exp(dmt - cmax)).astype(BF16)
                rt0 = _dot(vat, jnp.concatenate([jnp.where(lo_half, p0, zero), jnp.where(lo_half, zero, p0)], axis=0))
                dmax = jnp.max(bend + x, axis=0, keepdims=True)
                vw = (vat.astype(F32) * jnp.exp(bend + irow - brow - dmax)).astype(BF16)
                u0 = _dot(vw, jnp.concatenate([jnp.where(lo_half, kp, zero), jnp.where(lo_half, zero, kp)], axis=0))
                m = m_ref[st][0:1, :]
                cst = c_ref[st]
                inter = brow + m
                mt = jnp.maximum(inter, cmax)
                rt = jnp.exp(cmax - mt) * rt0 + jnp.exp(inter - mt) * _dot(cst.astype(BF16), qbd)
                hts.append(rt[0:ch] / jnp.maximum(jnp.abs(rt[ch:2 * ch]), jnp.exp(-mt)))
                mnew = jnp.maximum(bend + m, dmax)
                c_ref[st] = jnp.exp(bend + m - mnew) * cst + jnp.exp(dmax - mnew) * u0
                m_ref[st] = jnp.broadcast_to(mnew, (8, LANES))
            ht = jnp.concatenate(hts, axis=0).T
            h_ref[rows, 0:LANES] = ht[0:ch]
            h_ref[rows, LANES:2 * LANES] = ht[ch:2 * ch]


def _mlstm_layouts(mq, mv, g):
    n = mq.shape[0]
    nchunk = n // MLSTM_CHUNK
    qt = mq.T.reshape(256, nchunk, MLSTM_CHUNK)
    qtd = jnp.concatenate([qt, qt], axis=-1).reshape(256, nchunk * LANES)
    vt2 = mv.T.reshape(2, 2, HEAD_DIM, nchunk, MLSTM_CHUNK).transpose(0, 2, 3, 1, 4).reshape(2, HEAD_DIM, nchunk * LANES)
    gc = g[:, :16].reshape(nchunk, MLSTM_CHUNK, 16)
    rows = []
    for d in range(2):
        for p in range(2):
            ci = d * 8 + 2 * p
            irow = jnp.concatenate([gc[:, :, ci], gc[:, :, ci + 1]], axis=-1)
            brow = jnp.concatenate([gc[:, :, ci + 4], gc[:, :, ci + 5]], axis=-1)
            edge = MLSTM_CHUNK - 1 if d == 0 else 0
            bend = jnp.concatenate([jnp.broadcast_to(gc[:, edge, ci + 4 + e][:, None], (nchunk, MLSTM_CHUNK))
                                    for e in range(2)], axis=-1)
            rows += [irow, brow, bend]
    rows += [jnp.zeros_like(rows[0])] * 4
    return qtd, vt2, jnp.stack(rows, axis=0).reshape(16, nchunk * LANES)


def _mlstm_call(mk, qtd, vt2, g, gt3, *, b, rs, nsc, nsl, ncb):
    n = mk.shape[0]
    nch = rs // MLSTM_CHUNK

    def fwd(bi, j):
        return jnp.where(j < nsc, bi * nsc + j, ncb + bi * nsl + (j - nsc))

    def bwd(bi, j):
        return jnp.where(j < nsc, bi * nsc + (nsc - 1 - j), ncb + bi * nsl + (nsl - 1 - (j - nsc)))

    def specs(idx):
        return [pl.BlockSpec((rs, 256), lambda bi, j: (idx(bi, j), 0)),
                pl.BlockSpec((256, nch * LANES), lambda bi, j: (0, idx(bi, j))),
                pl.BlockSpec((2, HEAD_DIM, nch * LANES), lambda bi, j: (0, 0, idx(bi, j))),
                pl.BlockSpec((rs, LANES), lambda bi, j: (idx(bi, j), 0)),
                pl.BlockSpec((16, nch * LANES), lambda bi, j: (0, idx(bi, j)))]

    return pl.pallas_call(
        functools.partial(_mlstm_kernel, nch=nch),
        grid=(b, nsc + nsl),
        in_specs=specs(fwd) + specs(bwd),
        out_specs=[pl.BlockSpec((rs, 256), lambda bi, j: (fwd(bi, j), 0)),
                   pl.BlockSpec((rs, 256), lambda bi, j: (bwd(bi, j), 0))],
        out_shape=[jax.ShapeDtypeStruct((n, 256), F32)] * 2,
        scratch_shapes=[pltpu.VMEM((MLSTM_HEADS, LANES, LANES), F32),
                        pltpu.VMEM((MLSTM_HEADS, 8, LANES), F32)],
        compiler_params=_cparams(2, 32),
        name="mlstm",
    )(mk, qtd, vt2, g, gt3, mk, qtd, vt2, g, gt3)


def _attn_kernel(q_ref, k_ref, v_ref, o_ref, qs_ref, m_ref, aa_ref, ab_ref, *, tq, tc, t, tk, nqc):
    lane = lax.broadcasted_iota(jnp.int32, (tq, LANES), 1)
    lo_half = lane < HEAD_DIM
    for g in range(ATTN_KV_HEADS):
        for pp in range(2):
            qp = q_ref[:, (2 * g + pp) * LANES:(2 * g + pp + 1) * LANES]
            zero = jnp.zeros_like(qp)
            qs_ref[g, pp * tq:(pp + 1) * tq, :] = jnp.where(lo_half, qp, zero)
            qs_ref[g, (2 + pp) * tq:(3 + pp) * tq, :] = jnp.where(lo_half, zero, qp)
    m_ref[...] = jnp.full_like(m_ref, NEG_BIG)
    aa_ref[...] = jnp.zeros_like(aa_ref)
    ab_ref[...] = jnp.zeros_like(ab_ref)

    def chunk(start, size):
        for g in range(ATTN_KV_HEADS):
            kc = k_ref[0, pl.ds(start, size), g * LANES:(g + 1) * LANES]
            s = _dot_nt(qs_ref[g], kc)
            m_old = m_ref[g]
            m_new = jnp.maximum(m_old, jnp.max(s, axis=-1, keepdims=True))
            alpha = jnp.exp(m_old - m_new)
            p = jnp.exp(s - jnp.concatenate([m_new] * (size // LANES), axis=1)).astype(BF16)
            va = v_ref[0, pl.ds(start, size), (2 * g) * LANES:(2 * g + 1) * LANES]
            vb = v_ref[0, pl.ds(start, size), (2 * g + 1) * LANES:(2 * g + 2) * LANES]
            aa_ref[g] = alpha[:2 * tq] * aa_ref[g] + _dot(p[:2 * tq], va)
            ab_ref[g] = alpha[2 * tq:] * ab_ref[g] + _dot(p[2 * tq:], vb)
            m_ref[g] = m_new

    for c0 in range(0, tc, tk):
        chunk(c0, min(tk, tc - c0))

    @pl.when(pl.program_id(1) >= nqc)
    def _():
        def body(i, carry):
            chunk(pl.multiple_of(tc + i * tk, LANES), tk)
            return carry
        lax.fori_loop(0, t // tk, body, 0, unroll=2)

    for g in range(ATTN_KV_HEADS):
        for pp in range(2):
            a = aa_ref[g, pp * tq:(pp + 1) * tq, :]
            bq = ab_ref[g, pp * tq:(pp + 1) * tq, :]
            oa = a * pltpu.roll(1.0 / a, HEAD_DIM, 1)
            ob = bq * pltpu.roll(1.0 / bq, HEAD_DIM, 1)
            o_ref[:, (2 * g + pp) * LANES:(2 * g + pp + 1) * LANES] = jnp.where(lo_half, oa, ob)


def _attn_call(aq, kd, vd, *, b, tq, tc, t, tk, ncq):
    n = aq.shape[0]
    nqc, nql = tc // tq, t // tq

    def qidx(bi, j):
        return (jnp.where(j < nqc, bi * nqc + j, ncq + bi * nql + (j - nqc)), 0)

    return pl.pallas_call(
        functools.partial(_attn_kernel, tq=tq, tc=tc, t=t, tk=tk, nqc=nqc),
        grid=(b, nqc + nql),
        in_specs=[pl.BlockSpec((tq, 512), qidx),
                  pl.BlockSpec((1, tc + t, 256), lambda bi, j: (bi, 0, 0)),
                  pl.BlockSpec((1, tc + t, 512), lambda bi, j: (bi, 0, 0))],
        out_specs=pl.BlockSpec((tq, 512), qidx),
        out_shape=jax.ShapeDtypeStruct((n, 512), F32),
        scratch_shapes=[pltpu.VMEM((ATTN_KV_HEADS, 4 * tq, LANES), BF16),
                        pltpu.VMEM((ATTN_KV_HEADS, 4 * tq, LANES), F32),
                        pltpu.VMEM((ATTN_KV_HEADS, 2 * tq, LANES), F32),
                        pltpu.VMEM((ATTN_KV_HEADS, 2 * tq, LANES), F32)],
        compiler_params=_cparams(2, 48),
        name="attn",
    )(aq, kd, vd)


def _merge_kernel(hf_ref, hb_ref, mo_ref, gm_ref, a_ref, x_ref, mod_ref, mn_ref, wo_ref, rwh_ref, rwl_ref, rb_ref,
                  xo_ref, h2_ref, lg_ref):
    ind = _seg_indicator(2 * LANES)
    parts = [_sigmoid(mo_ref[...]) * (hf_ref[...] + hb_ref[...]), gm_ref[...],
             a_ref[:, 0:256], a_ref[:, 256:512]]
    y = jnp.concatenate([_head_rms(pt, ind) for pt in parts], axis=1) * mn_ref[...]
    xn = x_ref[...] + mod_ref[0, 2:3, :] * _dot(y.astype(BF16), wo_ref[...])
    xo_ref[...] = xn
    h2 = _rms(xn) * (1.0 + mod_ref[0, 4:5, :]) + mod_ref[0, 3:4, :]
    nsl = h2.shape[1] // LANES
    for sl in range(nsl):
        h2_ref[pl.ds(sl, h2.shape[0], stride=nsl), :] = h2[:, sl * LANES:(sl + 1) * LANES]
    hi, lo = _split(h2)
    lg_ref[...] = _dot(hi, rwh_ref[...]) + _dot(lo, rwh_ref[...]) + _dot(hi, rwl_ref[...]) + rb_ref[...]


def _merge_call(hf, hb, mo, gm, a, x, mods, mn, wo, rwh, rwl, rb, *, l, tm, nct, tpb):
    n, d = x.shape

    def mod_idx(i):
        return (l, jnp.where(i < nct, 0, 1 + (i - nct) // tpb), 0, 0)

    row = lambda i: (i, 0)
    lay = lambda i: (l, 0, 0)
    return pl.pallas_call(
        _merge_kernel,
        grid=(n // tm,),
        in_specs=[pl.BlockSpec((tm, 256), row), pl.BlockSpec((tm, 256), row), pl.BlockSpec((tm, 256), row),
                  pl.BlockSpec((tm, 256), row), pl.BlockSpec((tm, 512), row), pl.BlockSpec((tm, d), row),
                  pl.BlockSpec((None, 1, 6, d), mod_idx),
                  pl.BlockSpec((None, 1, d), lay),
                  pl.BlockSpec((None, d, d), lay),
                  pl.BlockSpec((None, d, LANES), lay),
                  pl.BlockSpec((None, d, LANES), lay),
                  pl.BlockSpec((None, 1, LANES), lay)],
        out_specs=[pl.BlockSpec((tm, d), row), pl.BlockSpec((tm * (d // LANES), LANES), row),
                   pl.BlockSpec((tm, LANES), row)],
        out_shape=[jax.ShapeDtypeStruct((n, d), F32), jax.ShapeDtypeStruct((n * (d // LANES), LANES), F32),
                   jax.ShapeDtypeStruct((n, LANES), F32)],
        compiler_params=_cparams(1, 48),
        name="merge",
    )(hf, hb, mo, gm, a, x, mods, mn, wo, rwh, rwl, rb)


def _wprep_kernel(w_ref, o_ref, *, f):
    slab = 2 * LANES
    r = lax.broadcasted_iota(jnp.int32, (slab, slab), 0)
    c = lax.broadcasted_iota(jnp.int32, (slab, slab), 1)
    perm = jnp.where(r == jnp.where(c < LANES, 2 * c, 2 * (c - LANES) + 1), 1.0, 0.0).astype(BF16)
    for s in range(2 * f // slab):
        y = _dot(w_ref[:, s * slab:(s + 1) * slab].astype(BF16), perm).astype(BF16)
        o_ref[:, s * LANES:(s + 1) * LANES] = y[:, :LANES]
        o_ref[:, f + s * LANES:f + (s + 1) * LANES] = y[:, LANES:]


def _wprep_call(w_gate_up):
    depth, ne, d, f2 = w_gate_up.shape
    rows = 512
    w2 = w_gate_up.reshape(depth * ne * d, f2)
    out = pl.pallas_call(
        functools.partial(_wprep_kernel, f=f2 // 2),
        grid=(depth * ne * d // rows,),
        in_specs=[pl.BlockSpec((rows, f2), lambda i: (i, 0))],
        out_specs=pl.BlockSpec((rows, f2), lambda i: (i, 0)),
        out_shape=jax.ShapeDtypeStruct((depth * ne * d, f2), BF16),
        compiler_params=_cparams(1, 32),
        name="wprep",
    )(w2)
    return out.reshape(depth, ne, d, f2)


def _moe_kernel(be_ref, nv_ref, tok0_ref, tokn_ref, h_hbm, wgl_ref, wd_ref, bgl_ref, bd_ref, y_ref,
                xbuf, act_ref, sem, *, tmb, f):
    i = pl.program_id(0)
    nv = nv_ref[0]
    slot = i % 2
    cw = 2 * LANES
    nchunk = f // cw
    per = tmb // (2 * nchunk)
    nsl = h_hbm.shape[1]

    def request(tok_ref, r, dst, priority=0):
        pltpu.make_async_copy(h_hbm.at[tok_ref[0, 0, r]], xbuf.at[dst, pl.ds(pl.multiple_of(r * nsl, nsl), nsl)],
                              sem.at[dst]).start(priority=priority)

    @pl.when(i == 0)
    def _():
        def issue(r, carry):
            request(tok0_ref, r, 0)
            return carry
        lax.fori_loop(0, tmb, issue, 0, unroll=8)

    @pl.when(i <= nv)
    def _():
        pltpu.make_async_copy(xbuf.at[slot], xbuf.at[slot], sem.at[slot]).wait()

    def compute(cur):
        xb = jnp.concatenate([xbuf[cur, pl.ds(sl, tmb, stride=nsl), :] for sl in range(nsl)], axis=1).astype(BF16)
        for j in range(nchunk):
            cols = slice(j * cw, (j + 1) * cw)
            ucols = slice(f + j * cw, f + (j + 1) * cw)
            x_glu = jnp.minimum(_dot(xb, wgl_ref[:, cols]) + bgl_ref[:, cols], SWIGLU_LIMIT)
            for r in range(2 * j * per, (2 * j + 1) * per):
                request(tokn_ref, r, 1 - cur, r % 2)
            x_lin = jnp.clip(_dot(xb, wgl_ref[:, ucols]) + bgl_ref[:, ucols], -SWIGLU_LIMIT, SWIGLU_LIMIT)
            for r in range((2 * j + 1) * per, (2 * j + 2) * per):
                request(tokn_ref, r, 1 - cur, r % 2)
            act_ref[:, cols] = (x_glu * _sigmoid(SWIGLU_ALPHA * x_glu) * (x_lin + 1.0)).astype(BF16)
        act = act_ref[...]
        for j in range(wd_ref.shape[1] // cw):
            cols = slice(j * cw, (j + 1) * cw)
            y = _dot(act, wd_ref[:, cols]) + bd_ref[:, cols]
            for sl in range(cw // LANES):
                y_ref[pl.ds(j * (cw // LANES) + sl, tmb, stride=nsl), :] = y[:, sl * LANES:(sl + 1) * LANES]

    for cur in range(2):
        pl.when(jnp.logical_and(i < nv, slot == cur))(functools.partial(compute, cur))

    @pl.when(i >= nv)
    def _():
        y_ref[...] = jnp.zeros_like(y_ref)


def _moe_call(block_expert, nvalid, slot_tok, h2, wgl, wd, bgl, bd, *, l, tmb):
    d = wd.shape[-1]
    nb = block_expert.shape[0]
    f = wd.shape[-2]
    wmap = lambda i, be, nv: (l, be[i], 0, 0)
    grid_spec = pltpu.PrefetchScalarGridSpec(
        num_scalar_prefetch=2,
        grid=(nb,),
        in_specs=[pl.BlockSpec((1, 1, tmb), lambda i, be, nv: (0, 0, 0), memory_space=pltpu.SMEM),
                  pl.BlockSpec((1, 1, tmb), lambda i, be, nv: (jnp.minimum(i + 1, nb - 1), 0, 0),
                               memory_space=pltpu.SMEM),
                  pl.BlockSpec(memory_space=pl.ANY),
                  pl.BlockSpec((None, None, d, 2 * f), wmap),
                  pl.BlockSpec((None, None, f, d), wmap),
                  pl.BlockSpec((None, None, 1, 2 * f), wmap),
                  pl.BlockSpec((None, None, 1, d), wmap)],
        out_specs=pl.BlockSpec((tmb * (d // LANES), LANES), lambda i, be, nv: (i, 0)),
        scratch_shapes=[pltpu.VMEM((2, tmb * (d // LANES), LANES), F32), pltpu.VMEM((tmb, f), BF16),
                        pltpu.SemaphoreType.DMA((2,))],
    )
    return pl.pallas_call(
        functools.partial(_moe_kernel, tmb=tmb, f=f),
        grid_spec=grid_spec,
        out_shape=jax.ShapeDtypeStruct((nb * tmb * (d // LANES), LANES), F32),
        compiler_params=_cparams(1, 48),
        name="moe",
    )(block_expert, nvalid, slot_tok, slot_tok, h2, wgl, wd, bgl, bd)


def _combine_kernel(dest0_ref, destn_ref, y_hbm, gate_ref, x_ref, mod_ref, o_ref, buf, sem, *, tt):
    i = pl.program_id(0)
    slot = i % 2
    nsl = y_hbm.shape[1]

    def request_tile(dest_ref, dst):
        def issue(r, carry):
            for k in range(TOP_K):
                pltpu.make_async_copy(y_hbm.at[dest_ref[0, 0, r * TOP_K + k]],
                                      buf.at[dst, k, pl.ds(pl.multiple_of(r * nsl, nsl), nsl)],
                                      sem.at[dst]).start(priority=k % 2)
            return carry
        lax.fori_loop(0, tt, issue, 0, unroll=4)

    @pl.when(i == 0)
    def _():
        request_tile(dest0_ref, 0)

    @pl.when(i + 1 < pl.num_programs(0))
    def _():
        request_tile(destn_ref, 1 - slot)

    for k in range(TOP_K):
        pltpu.make_async_copy(buf.at[slot, k], buf.at[slot, k], sem.at[slot]).wait()
    gate = gate_ref[...]
    f = None
    for k in range(TOP_K):
        rows = jnp.concatenate([buf[slot, k, pl.ds(sl, tt, stride=nsl), :] for sl in range(nsl)], axis=1)
        f = gate[:, k:k + 1] * rows if f is None else f + gate[:, k:k + 1] * rows
    o_ref[...] = x_ref[...] + mod_ref[0, 5:6, :] * f


def _combine_call(dest, y_slots, gates, x, mods, *, l, tt, nct, tpb):
    n, d = x.shape
    ntile = n // tt

    def mod_idx(i):
        return (l, jnp.where(i < nct, 0, 1 + (i - nct) // tpb), 0, 0)

    row = lambda i: (i, 0)
    return pl.pallas_call(
        functools.partial(_combine_kernel, tt=tt),
        grid=(ntile,),
        in_specs=[pl.BlockSpec((1, 1, tt * TOP_K), lambda i: (0, 0, 0), memory_space=pltpu.SMEM),
                  pl.BlockSpec((1, 1, tt * TOP_K), lambda i: (jnp.minimum(i + 1, ntile - 1), 0, 0),
                               memory_space=pltpu.SMEM),
                  pl.BlockSpec(memory_space=pl.ANY),
                  pl.BlockSpec((tt, LANES), row),
                  pl.BlockSpec((tt, d), row),
                  pl.BlockSpec((None, 1, 6, d), mod_idx)],
        out_specs=pl.BlockSpec((tt, d), row),
        out_shape=jax.ShapeDtypeStruct((n, d), F32),
        scratch_shapes=[pltpu.VMEM((2, TOP_K, tt * (d // LANES), LANES), F32), pltpu.SemaphoreType.DMA((2,))],
        compiler_params=_cparams(1, 32),
        name="combine",
    )(dest, dest, y_slots, gates, x, mods)


def _final_kernel(x_ref, g_ref, o_ref):
    o_ref[...] = _rms(x_ref[...]) * g_ref[...]


def _final_call(x, gain, *, tm, nct, nl):
    d = x.shape[1]
    return pl.pallas_call(
        _final_kernel,
        grid=(nl // tm,),
        in_specs=[pl.BlockSpec((tm, d), lambda i: (nct + i, 0)), pl.BlockSpec((1, d), lambda i: (0, 0))],
        out_specs=pl.BlockSpec((tm, d), lambda i: (i, 0)),
        out_shape=jax.ShapeDtypeStruct((nl, d), F32),
        compiler_params=_cparams(1, 32),
        name="final_norm",
    )(x, gain.reshape(1, d))


def _route(logits, tmb, n_blocks):
    n = logits.shape[0]
    top_val, top_idx = lax.top_k(logits[:, :N_EXPERTS], TOP_K)
    gates = jax.nn.softmax(top_val, axis=-1)
    onehot = jnp.sum(jax.nn.one_hot(top_idx, N_EXPERTS, dtype=jnp.int32), axis=1)
    before = jnp.cumsum(onehot, axis=0) - onehot
    counts = before[-1] + onehot[-1]
    padded = (counts + tmb - 1) // tmb * tmb
    ends = jnp.cumsum(padded)
    dest = (ends - padded)[top_idx] + jnp.take_along_axis(before, top_idx, axis=1)
    nvalid = (ends[-1] // tmb).astype(jnp.int32)
    blk = jnp.minimum(jnp.arange(n_blocks, dtype=jnp.int32), nvalid - 1)
    block_expert = jnp.minimum(jnp.searchsorted(ends, blk * tmb, side='right'), N_EXPERTS - 1).astype(jnp.int32)
    nfill = n_blocks * tmb - n * TOP_K
    pad = padded - counts
    padcum = jnp.cumsum(pad)
    j = jnp.arange(nfill, dtype=jnp.int32)
    fe = jnp.sum(j[:, None] >= padcum[None, :], axis=1)
    first_free = jnp.concatenate([ends - pad, ends[-1:]])
    fill_slot = first_free[fe] + j - jnp.concatenate([padcum - pad, padcum[-1:]])[fe]
    tok = jnp.broadcast_to(jnp.arange(n, dtype=jnp.int32)[:, None], (n, TOP_K)).reshape(-1)
    _, slot_tok = lax.sort((jnp.concatenate([dest.reshape(-1), fill_slot]).astype(jnp.int32),
                            jnp.concatenate([tok, jnp.zeros((nfill,), jnp.int32)])), num_keys=1)
    return gates, dest.astype(jnp.int32), slot_tok, block_expert, nvalid.reshape(1)


def _rope_tables(t, tm):
    rows = t // GRID_W
    row = jnp.repeat(jnp.arange(rows, dtype=F32), GRID_W)
    col = jnp.tile(jnp.arange(GRID_W, dtype=F32), rows)
    npair = HEAD_DIM // 4
    inv_freq = ROPE_THETA ** (-jnp.arange(npair, dtype=F32) / npair)
    ang = jnp.concatenate([row[:, None] * inv_freq, col[:, None] * inv_freq], axis=-1)
    cos = jnp.tile(jnp.cos(ang), (1, 4))
    sin = jnp.tile(jnp.concatenate([-jnp.sin(ang), jnp.sin(ang)], axis=-1), (1, 2))
    cos = jnp.concatenate([jnp.ones((tm, LANES), F32), cos], axis=0)
    sin = jnp.concatenate([jnp.zeros((tm, LANES), F32), sin], axis=0)
    return cos, sin


def _tile(nc, t, cap):
    for cand in (1024, 512, 256, 128):
        if cand <= cap and nc % cand == 0 and t % cand == 0:
            return cand
    raise ValueError("unsupported sequence lengths")


def kernel(x, c, ctx, c_ctx, w_ada, b_ada, w_in, b_in, q_norm, k_norm, gmlp_norm, gmlp_ws, gmlp_b, mix_norm, w_out,
           router_w, router_b, w_gate_up, b_gate_up, w_down, b_down, final_norm):
    b, t, d = x.shape
    tc = ctx.shape[1]
    depth = w_in.shape[0]
    nc, nl = b * tc, b * t
    n = nc + nl
    tm = _tile(nc, t, 512)
    rs = _tile(tc, t, 256)
    tq = _tile(tc, t, 256)
    tk = _tile(t, t, 512)
    tmb = 512
    tt = _tile(nc, t, 256)
    nct, tpb = nc // tm, t // tm
    n_blocks = -(-n * TOP_K // tmb) + N_EXPERTS

    def heads_0213(a, axis):
        blk = [lax.slice_in_dim(a, i * HEAD_DIM, (i + 1) * HEAD_DIM, axis=axis) for i in (0, 2, 1, 3)]
        rest = lax.slice_in_dim(a, MLSTM_HEADS * HEAD_DIM, a.shape[axis], axis=axis)
        return jnp.concatenate(blk + [rest], axis=axis)

    def cols(a):
        pad = jnp.zeros(a.shape[:-1] + (LANES - 16,), a.dtype)
        return jnp.concatenate([a[..., 0:768], heads_0213(a[..., 768:1024], a.ndim - 1), a[..., 1040:2320],
                                a[..., 1024:1040], pad], axis=-1)

    w_in_r = cols(w_in).astype(BF16)
    b_in_r = cols(b_in).reshape(depth, 1, C_END)
    qn = jnp.tile(q_norm, (1, ATTN_HEADS)).reshape(depth, 1, 512)
    kn = jnp.tile(k_norm, (1, 2)).reshape(depth, 1, LANES)
    gn = gmlp_norm.reshape(depth, 1, 256)
    ws4 = gmlp_ws.reshape(depth, GMLP_GROUPS * GMLP_CHUNK, GMLP_CHUNK).astype(BF16)
    gb = jnp.repeat(jnp.swapaxes(gmlp_b, 1, 2), HEAD_DIM, axis=2)
    mn = heads_0213(mix_norm, 1).reshape(depth, 1, d)
    wo = heads_0213(w_out, 1).astype(BF16)
    rw = jnp.pad(router_w, ((0, 0), (0, 0), (0, LANES - N_EXPERTS)))
    rwh = rw.astype(BF16)
    rwl = (rw - rwh.astype(F32)).astype(BF16)
    rb = jnp.pad(router_b, ((0, 0), (0, LANES - N_EXPERTS))).reshape(depth, 1, LANES)
    wgl = _wprep_call(w_gate_up)
    wd = w_down.astype(BF16)
    ne, ff = w_down.shape[1], w_down.shape[2]
    bgl = jnp.concatenate([b_gate_up[..., 0::2], b_gate_up[..., 1::2]], axis=-1).reshape(depth, ne, 1, 2 * ff)
    bd = b_down.reshape(depth, ne, 1, d)
    cos_t, sin_t = _rope_tables(t, tm)

    r = -(-(b + 1) // 8) * 8
    cs = jnp.zeros((r, d), F32).at[0].set(c_ctx).at[1:b + 1].set(c)
    mods = _ada_call(cs, w_ada, b_ada).reshape(depth, r, 6, d)

    xs = jnp.concatenate([ctx.reshape(nc, d), x.reshape(nl, d)], axis=0)
    for l in range(depth):
        mq, mk, mv, mo, g, gm, aq, akd, avd = _proj_call(
            xs, mods, w_in_r, b_in_r, qn, kn, gn, ws4, gb, cos_t, sin_t, l=l, tm=tm, nct=nct, tpb=tpb)
        qtd, vt2, gt3 = _mlstm_layouts(mq, mv, g)
        hf, hb = _mlstm_call(mk, qtd, vt2, g, gt3, b=b, rs=rs, nsc=tc // rs, nsl=t // rs, ncb=nc // rs)
        kd = jnp.concatenate([akd[:nc].reshape(b, tc, 256), akd[nc:].reshape(b, t, 256)], axis=1)
        vd = jnp.concatenate([avd[:nc].reshape(b, tc, 512), avd[nc:].reshape(b, t, 512)], axis=1)
        a = _attn_call(aq, kd, vd, b=b, tq=tq, tc=tc, t=t, tk=tk, ncq=nc // tq)
        xs, h2, logits = _merge_call(hf, hb, mo, gm, a, xs, mods, mn, wo, rwh, rwl, rb, l=l, tm=tm, nct=nct, tpb=tpb)
        gates, dest, slot_tok, block_expert, nvalid = _route(logits, tmb, n_blocks)
        y_slots = _moe_call(block_expert, nvalid, slot_tok.reshape(n_blocks, 1, tmb),
                            h2.reshape(n, d // LANES, LANES), wgl, wd, bgl, bd, l=l, tmb=tmb)
        xs = _combine_call(dest.reshape(n // tt, 1, tt * TOP_K), y_slots.reshape(n_blocks * tmb, d // LANES, LANES),
                           jnp.pad(gates, ((0, 0), (0, LANES - TOP_K))), xs, mods, l=l, tt=tt,
                           nct=nc // tt, tpb=t // tt)
    out = _final_call(xs, final_norm, tm=tm, nct=nct, nl=nl)
    return out.reshape(b, t, d)
```

```python
import functools

import jax
import jax.numpy as jnp
from jax import lax
from jax.experimental import pallas as pl
from jax.experimental.pallas import tpu as pltpu

F32 = jnp.float32
BF16 = jnp.bfloat16

HEAD_DIM = 64
EPS = 1e-6
GRID_W = 64
ROPE_THETA = 10000.0
MLSTM_HEADS = 4
MLSTM_CHUNK = 64
GMLP_GROUPS = 4
GMLP_CHUNK = 128
ATTN_HEADS = 8
ATTN_KV_HEADS = 2
N_EXPERTS = 32
TOP_K = 4
SWIGLU_LIMIT = 7.0
SWIGLU_ALPHA = 1.702
LANES = 128
NEG_BIG = -1e30

C_MQ, C_MK, C_MV, C_MO, C_GU, C_GV, C_AQ, C_AK, C_AV, C_G, C_END = (
    0, 256, 512, 768, 1024, 1280, 1536, 2048, 2176, 2304, 2432)


def _dot(a, b):
    return jnp.dot(a, b, preferred_element_type=F32)


def _dot_nt(a, b):
    return lax.dot_general(a, b, (((1,), (1,)), ((), ())), preferred_element_type=F32)


def _split(x):
    hi = x.astype(BF16)
    lo = (x - hi.astype(F32)).astype(BF16)
    return hi, lo


def _cparams(ndims, vmem_mb):
    return pltpu.CompilerParams(dimension_semantics=("arbitrary",) * ndims,
                                vmem_limit_bytes=vmem_mb << 20)


def _sigmoid(x):
    return 1.0 / (1.0 + jnp.exp(-x))


def _gelu_tanh(x):
    return 0.5 * x * (1.0 + jnp.tanh(0.7978845608028654 * (x + 0.044715 * (x * x * x))))


def _rms(x):
    return x * lax.rsqrt(jnp.mean(x * x, axis=-1, keepdims=True) + EPS)


def _seg_indicator(n):
    r = lax.broadcasted_iota(jnp.int32, (n, n), 0) // HEAD_DIM
    c = lax.broadcasted_iota(jnp.int32, (n, n), 1) // HEAD_DIM
    return jnp.where(r == c, 1.0, 0.0).astype(BF16)


def _head_rms(x, ind):
    hi, lo = _split(x * x)
    ss = _dot(hi, ind) + _dot(lo, ind)
    return x * lax.rsqrt(ss * (1.0 / HEAD_DIM) + EPS)


def _ada_kernel(c_ref, w_ref, b_ref, o_ref):
    c = c_ref[...]
    s = c * _sigmoid(c)
    s_hi, s_lo = _split(s)
    w_hi, w_lo = _split(w_ref[0])
    o_ref[0] = _dot(s_hi, w_hi) + _dot(s_lo, w_hi) + _dot(s_hi, w_lo) + b_ref[0]


def _ada_call(cs, w_ada, b_ada):
    depth, d, d6 = w_ada.shape
    r = cs.shape[0]
    tn = 1536
    return pl.pallas_call(
        _ada_kernel,
        grid=(depth, d6 // tn),
        in_specs=[pl.BlockSpec((r, d), lambda l, j: (0, 0)),
                  pl.BlockSpec((1, d, tn), lambda l, j: (l, 0, j)),
                  pl.BlockSpec((1, 1, tn), lambda l, j: (l, 0, j))],
        out_specs=pl.BlockSpec((1, r, tn), lambda l, j: (l, 0, j)),
        out_shape=jax.ShapeDtypeStruct((depth, r, d6), F32),
        compiler_params=_cparams(2, 40),
        name="ada",
    )(cs, w_ada, b_ada.reshape(depth, 1, d6))


def _proj_kernel(x_ref, mod_ref, w_ref, b_ref, qn_ref, kn_ref, gn_ref, ws_ref, gb_ref, cos_ref, sin_ref,
                 qt_ref, mk_ref, vt_ref, mo_ref, g_ref, gm_ref, aq_ref, ak_ref, av_ref, *, tm):
    h = (_rms(x_ref[...]) * (1.0 + mod_ref[0, 1:2, :]) + mod_ref[0, 0:1, :]).astype(BF16)

    def seg(a, b):
        return _dot(h, w_ref[:, a:b]) + b_ref[:, a:b]

    lo_q = lax.broadcasted_iota(jnp.int32, (2 * LANES, LANES), 1) < MLSTM_CHUNK
    lo_v = lax.broadcasted_iota(jnp.int32, (HEAD_DIM, LANES), 1) < MLSTM_CHUNK
    qt = seg(C_MQ, C_MK).T
    vt = seg(C_MV, C_MO).T
    for m in range(tm // LANES):
        even, odd = slice(2 * m * LANES, (2 * m + 1) * LANES), slice((2 * m + 1) * LANES, (2 * m + 2) * LANES)
        qv = qt[:, m * LANES:(m + 1) * LANES]
        qr = pltpu.roll(qv, MLSTM_CHUNK, 1)
        qt_ref[:, even] = jnp.where(lo_q, qv, qr).astype(BF16)
        qt_ref[:, odd] = jnp.where(lo_q, qr, qv).astype(BF16)
        for p in range(MLSTM_HEADS // 2):
            va = vt[p * LANES:p * LANES + HEAD_DIM, m * LANES:(m + 1) * LANES]
            vb = vt[p * LANES + HEAD_DIM:(p + 1) * LANES, m * LANES:(m + 1) * LANES]
            vt_ref[p, :, even] = jnp.where(lo_v, va, pltpu.roll(vb, MLSTM_CHUNK, 1)).astype(BF16)
            vt_ref[p, :, odd] = jnp.where(lo_v, pltpu.roll(va, MLSTM_CHUNK, 1), vb).astype(BF16)
    mk_ref[...] = (seg(C_MK, C_MV) * HEAD_DIM ** -0.5).astype(BF16)
    mo_ref[...] = seg(C_MO, C_GU)

    g = seg(C_G, C_END)
    ls = jnp.minimum(g, 0.0) - jnp.log1p(jnp.exp(-jnp.abs(g)))
    r = lax.broadcasted_iota(jnp.int32, (tm, tm), 0)
    c = lax.broadcasted_iota(jnp.int32, (tm, tm), 1)
    same = (r // MLSTM_CHUNK) == (c // MLSTM_CHUNK)
    pre = jnp.where(same, jnp.where(c <= r, 1.0, 0.0), 0.0).astype(BF16)
    suf = jnp.where(same, jnp.where(c >= r, 1.0, 0.0), 0.0).astype(BF16)
    ls_hi, ls_lo = _split(ls)
    psum = _dot(pre, ls_hi) + _dot(pre, ls_lo)
    ssum = _dot(suf, ls_hi) + _dot(suf, ls_lo)
    lane = lax.broadcasted_iota(jnp.int32, (tm, LANES), 1)
    q4 = lane // MLSTM_HEADS
    g_ref[...] = jnp.where(q4 == 1, psum, jnp.where(q4 == 3, ssum, g))

    u = _gelu_tanh(seg(C_GU, C_GV))
    v = (_rms(_gelu_tanh(seg(C_GV, C_AQ))) * gn_ref[...]).astype(BF16)
    grp = lax.broadcasted_iota(jnp.int32, (GMLP_CHUNK, 2 * LANES), 1) // HEAD_DIM
    for ci in range(tm // GMLP_CHUNK):
        rows = slice(ci * GMLP_CHUNK, (ci + 1) * GMLP_CHUNK)
        full = _dot(ws_ref[...], v[rows, :])
        mixed = gb_ref[...]
        for gi in range(GMLP_GROUPS):
            mixed = mixed + jnp.where(grp == gi, full[gi * GMLP_CHUNK:(gi + 1) * GMLP_CHUNK, :], 0.0)
        gm_ref[rows, :] = u[rows, :] * mixed

    cos = cos_ref[...]
    sin = sin_ref[...]
    first_half = (lane % HEAD_DIM) < (HEAD_DIM // 2)
    lo_half = lane < HEAD_DIM

    def rope(blk):
        partner = jnp.where(first_half, pltpu.roll(blk, LANES - HEAD_DIM // 2, 1),
                            pltpu.roll(blk, HEAD_DIM // 2, 1))
        return blk * cos + partner * sin

    ind256 = _seg_indicator(2 * LANES)
    for s in range(2):
        qn = _head_rms(seg(C_AQ + s * 256, C_AQ + (s + 1) * 256), ind256) * qn_ref[:, s * 256:(s + 1) * 256]
        for j in range(2):
            blk = rope(qn[:, j * LANES:(j + 1) * LANES]) * HEAD_DIM ** -0.5
            aq_ref[:, s * 256 + j * LANES: s * 256 + (j + 1) * LANES] = blk.astype(BF16)

    kr = rope(_head_rms(seg(C_AK, C_AV), ind256[:LANES, :LANES]) * kn_ref[...])
    kroll = pltpu.roll(kr, HEAD_DIM, 1)
    ak_ref[:, 0:LANES] = jnp.where(lo_half, kr, kroll).astype(BF16)
    ak_ref[:, LANES:2 * LANES] = jnp.where(lo_half, kroll, kr).astype(BF16)

    vf = seg(C_AV, C_G)
    vroll = pltpu.roll(vf, HEAD_DIM, 1)
    av_ref[:, 0:LANES] = jnp.where(lo_half, vf, 1.0).astype(BF16)
    av_ref[:, LANES:2 * LANES] = jnp.where(lo_half, 1.0, vroll).astype(BF16)
    av_ref[:, 2 * LANES:3 * LANES] = jnp.where(lo_half, vroll, 1.0).astype(BF16)
    av_ref[:, 3 * LANES:4 * LANES] = jnp.where(lo_half, 1.0, vf).astype(BF16)


def _proj_call(x, mods, w, b, qn, kn, gn, ws4, gb, cos_t, sin_t, *, l, tm, nct, tpb):
    n, d = x.shape

    def mod_idx(i):
        return (l, jnp.where(i < nct, 0, 1 + (i - nct) // tpb), 0, 0)

    def rope_idx(i):
        return (jnp.where(i < nct, 0, 1 + (i - nct) % tpb), 0)

    row = lambda i: (i, 0)
    const2 = lambda i: (0, 0)
    lay = lambda i: (l, 0, 0)
    nlane = tm // MLSTM_CHUNK * LANES
    outs = [(256, BF16), (256, F32), (LANES, F32), (256, F32), (512, BF16), (256, BF16), (512, BF16)]
    row_specs = [pl.BlockSpec((tm, wd), row) for wd, _ in outs]
    row_shapes = [jax.ShapeDtypeStruct((n, wd), dt) for wd, dt in outs]
    qt_spec = pl.BlockSpec((256, nlane), lambda i: (0, i))
    vt_spec = pl.BlockSpec((MLSTM_HEADS // 2, HEAD_DIM, nlane), lambda i: (0, 0, i))
    qt_shape = jax.ShapeDtypeStruct((256, n // MLSTM_CHUNK * LANES), BF16)
    vt_shape = jax.ShapeDtypeStruct((MLSTM_HEADS // 2, HEAD_DIM, n // MLSTM_CHUNK * LANES), BF16)
    return pl.pallas_call(
        functools.partial(_proj_kernel, tm=tm),
        grid=(n // tm,),
        in_specs=[pl.BlockSpec((tm, d), row),
                  pl.BlockSpec((None, 1, 6, d), mod_idx),
                  pl.BlockSpec((None, d, C_END), lay),
                  pl.BlockSpec((None, 1, C_END), lay),
                  pl.BlockSpec((None, 1, 512), lay),
                  pl.BlockSpec((None, 1, LANES), lay),
                  pl.BlockSpec((None, 1, 256), lay),
                  pl.BlockSpec((None, GMLP_GROUPS * GMLP_CHUNK, GMLP_CHUNK), lay),
                  pl.BlockSpec((None, GMLP_CHUNK, 256), lay),
                  pl.BlockSpec((tm, LANES), rope_idx),
                  pl.BlockSpec((tm, LANES), rope_idx)],
        out_specs=[qt_spec, row_specs[0], vt_spec] + row_specs[1:],
        out_shape=[qt_shape, row_shapes[0], vt_shape] + row_shapes[1:],
        compiler_params=_cparams(1, 48),
        name="proj",
    )(x, mods, w, b, qn, kn, gn, ws4, gb, cos_t, sin_t)


def _mlstm_kernel(kf_ref, qf_ref, vf_ref, gf_ref, gtf_ref, kb_ref, qb_ref, vb_ref, gb_ref, gtb_ref,
                  hf_ref, hb_ref, c_ref, m_ref, *, nch):
    ch = MLSTM_CHUNK

    @pl.when(pl.program_id(1) == 0)
    def _():
        c_ref[...] = jnp.zeros_like(c_ref)
        m_ref[...] = jnp.zeros_like(m_ref)

    sub = lax.broadcasted_iota(jnp.int32, (ch, LANES), 0)
    lane = lax.broadcasted_iota(jnp.int32, (ch, LANES), 1)
    lo_half = lane < ch
    pos = lane % ch
    r2 = lax.broadcasted_iota(jnp.int32, (LANES, LANES), 0)
    c2 = lax.broadcasted_iota(jnp.int32, (LANES, LANES), 1)
    blockdiag = (r2 < ch) == (c2 < ch)
    ones_rows = jnp.ones((ch, LANES), BF16)
    dirs = ((kf_ref, qf_ref, vf_ref, gf_ref, gtf_ref, hf_ref), (kb_ref, qb_ref, vb_ref, gb_ref, gtb_ref, hb_ref))
    for d, (k_ref, q_ref, v_ref, g_ref, gt_ref, h_ref) in enumerate(dirs):
        mask = (sub <= pos) if d == 0 else (sub >= pos)
        sels = []
        for p in range(MLSTM_HEADS // 2):
            gi = d * 2 * MLSTM_HEADS + 2 * p + jnp.where(c2 >= ch, 1, 0)
            sels.append((jnp.where(r2 == gi, 1.0, 0.0) - jnp.where(r2 == gi + MLSTM_HEADS, 1.0, 0.0)).astype(BF16))
        order = range(nch) if d == 0 else range(nch - 1, -1, -1)
        for ci in order:
            rows = slice(ci * ch, (ci + 1) * ch)
            cl = slice(ci * LANES, (ci + 1) * LANES)
            g = g_ref[rows, :]
            g_hi = g.astype(BF16)
            g_r = g - g_hi.astype(F32)
            g_mid = g_r.astype(BF16)
            g_lo = (g_r - g_mid.astype(F32)).astype(BF16)
            hts = []
            for p in range(MLSTM_HEADS // 2):
                st = d * (MLSTM_HEADS // 2) + p
                kp = k_ref[rows, p * LANES:(p + 1) * LANES]
                zero = jnp.zeros_like(kp)
                qbd = jnp.where(blockdiag, q_ref[p * LANES:(p + 1) * LANES, cl], jnp.zeros((LANES, LANES), BF16))
                vat = jnp.concatenate([v_ref[p, :, cl], ones_rows], axis=0)
                irow = gt_ref[3 * st:3 * st + 1, cl]
                brow = gt_ref[3 * st + 1:3 * st + 2, cl]
                bend = gt_ref[3 * st + 2:3 * st + 3, cl]
                x = _dot(g_hi, sels[p]) + _dot(g_mid, sels[p]) + _dot(g_lo, sels[p])
                dmt = jnp.where(mask, brow + x, -jnp.inf)
                cmax = jnp.max(dmt, axis=0, keepdims=True)
                p0 = (_dot(kp, qbd) * jnp.exp(dmt - cmax)).astype(BF16)
                rt0 = _dot(vat, jnp.concatenate([jnp.where(lo_half, p0, zero), jnp.where(lo_half, zero, p0)], axis=0))
                dmax = jnp.max(bend + x, axis=0, keepdims=True)
                vw = (vat.astype(F32) * jnp.exp(bend + irow - brow - dmax)).astype(BF16)
                u0 = _dot(vw, jnp.concatenate([jnp.where(lo_half, kp, zero), jnp.where(lo_half, zero, kp)], axis=0))
                m = m_ref[st][0:1, :]
                cst = c_ref[st]
                inter = brow + m
                mt = jnp.maximum(inter, cmax)
                rt = jnp.exp(cmax - mt) * rt0 + jnp.exp(inter - mt) * _dot(cst.astype(BF16), qbd)
                hts.append(rt[0:ch] / jnp.maximum(jnp.abs(rt[ch:2 * ch]), jnp.exp(-mt)))
                mnew = jnp.maximum(bend + m, dmax)
                c_ref[st] = jnp.exp(bend + m - mnew) * cst + jnp.exp(dmax - mnew) * u0
                m_ref[st] = jnp.broadcast_to(mnew, (8, LANES))
            ht = jnp.concatenate(hts, axis=0).T
            h_ref[rows, 0:LANES] = ht[0:ch]
            h_ref[rows, LANES:2 * LANES] = ht[ch:2 * ch]


def _mlstm_gate_rows(g):
    nchunk = g.shape[0] // MLSTM_CHUNK
    gc = g[:, :16].reshape(nchunk, MLSTM_CHUNK, 16)
    rows = []
    for d in range(2):
        for p in range(2):
            ci = d * 8 + 2 * p
            irow = jnp.concatenate([gc[:, :, ci], gc[:, :, ci + 1]], axis=-1)
            brow = jnp.concatenate([gc[:, :, ci + 4], gc[:, :, ci + 5]], axis=-1)
            edge = MLSTM_CHUNK - 1 if d == 0 else 0
            bend = jnp.concatenate([jnp.broadcast_to(gc[:, edge, ci + 4 + e][:, None], (nchunk, MLSTM_CHUNK))
                                    for e in range(2)], axis=-1)
            rows += [irow, brow, bend]
    rows += [jnp.zeros_like(rows[0])] * 4
    return jnp.stack(rows, axis=0).reshape(16, nchunk * LANES)


def _mlstm_call(mk, qtd, vt2, g, gt3, *, b, rs, nsc, nsl, ncb):
    n = mk.shape[0]
    nch = rs // MLSTM_CHUNK

    def fwd(bi, j):
        return jnp.where(j < nsc, bi * nsc + j, ncb + bi * nsl + (j - nsc))

    def bwd(bi, j):
        return jnp.where(j < nsc, bi * nsc + (nsc - 1 - j), ncb + bi * nsl + (nsl - 1 - (j - nsc)))

    def specs(idx):
        return [pl.BlockSpec((rs, 256), lambda bi, j: (idx(bi, j), 0)),
                pl.BlockSpec((256, nch * LANES), lambda bi, j: (0, idx(bi, j))),
                pl.BlockSpec((2, HEAD_DIM, nch * LANES), lambda bi, j: (0, 0, idx(bi, j))),
                pl.BlockSpec((rs, LANES), lambda bi, j: (idx(bi, j), 0)),
                pl.BlockSpec((16, nch * LANES), lambda bi, j: (0, idx(bi, j)))]

    return pl.pallas_call(
        functools.partial(_mlstm_kernel, nch=nch),
        grid=(b, nsc + nsl),
        in_specs=specs(fwd) + specs(bwd),
        out_specs=[pl.BlockSpec((rs, 256), lambda bi, j: (fwd(bi, j), 0)),
                   pl.BlockSpec((rs, 256), lambda bi, j: (bwd(bi, j), 0))],
        out_shape=[jax.ShapeDtypeStruct((n, 256), F32)] * 2,
        scratch_shapes=[pltpu.VMEM((MLSTM_HEADS, LANES, LANES), F32),
                        pltpu.VMEM((MLSTM_HEADS, 8, LANES), F32)],
        compiler_params=_cparams(2, 32),
        name="mlstm",
    )(mk, qtd, vt2, g, gt3, mk, qtd, vt2, g, gt3)


def _attn_kernel(q_ref, k_ref, v_ref, o_ref, qs_ref, m_ref, aa_ref, ab_ref, *, tq, tc, t, tk, nqc):
    lane = lax.broadcasted_iota(jnp.int32, (tq, LANES), 1)
    lo_half = lane < HEAD_DIM
    for g in range(ATTN_KV_HEADS):
        for pp in range(2):
            qp = q_ref[:, (2 * g + pp) * LANES:(2 * g + pp + 1) * LANES]
            zero = jnp.zeros_like(qp)
            qs_ref[g, pp * tq:(pp + 1) * tq, :] = jnp.where(lo_half, qp, zero)
            qs_ref[g, (2 + pp) * tq:(3 + pp) * tq, :] = jnp.where(lo_half, zero, qp)
    m_ref[...] = jnp.full_like(m_ref, NEG_BIG)
    aa_ref[...] = jnp.zeros_like(aa_ref)
    ab_ref[...] = jnp.zeros_like(ab_ref)

    def chunk(start, size):
        for g in range(ATTN_KV_HEADS):
            kc = k_ref[0, pl.ds(start, size), g * LANES:(g + 1) * LANES]
            s = _dot_nt(qs_ref[g], kc)
            m_old = m_ref[g]
            m_new = jnp.maximum(m_old, jnp.max(s, axis=-1, keepdims=True))
            alpha = jnp.exp(m_old - m_new)
            p = jnp.exp(s - jnp.concatenate([m_new] * (size // LANES), axis=1)).astype(BF16)
            va = v_ref[0, pl.ds(start, size), (2 * g) * LANES:(2 * g + 1) * LANES]
            vb = v_ref[0, pl.ds(start, size), (2 * g + 1) * LANES:(2 * g + 2) * LANES]
            aa_ref[g] = alpha[:2 * tq] * aa_ref[g] + _dot(p[:2 * tq], va)
            ab_ref[g] = alpha[2 * tq:] * ab_ref[g] + _dot(p[2 * tq:], vb)
            m_ref[g] = m_new

    for c0 in range(0, tc, tk):
        chunk(c0, min(tk, tc - c0))

    @pl.when(pl.program_id(1) >= nqc)
    def _():
        def body(i, carry):
            chunk(pl.multiple_of(tc + i * tk, LANES), tk)
            return carry
        lax.fori_loop(0, t // tk, body, 0, unroll=2)

    for g in range(ATTN_KV_HEADS):
        for pp in range(2):
            a = aa_ref[g, pp * tq:(pp + 1) * tq, :]
            bq = ab_ref[g, pp * tq:(pp + 1) * tq, :]
            oa = a * pltpu.roll(1.0 / a, HEAD_DIM, 1)
            ob = bq * pltpu.roll(1.0 / bq, HEAD_DIM, 1)
            o_ref[:, (2 * g + pp) * LANES:(2 * g + pp + 1) * LANES] = jnp.where(lo_half, oa, ob)


def _attn_call(aq, kd, vd, *, b, tq, tc, t, tk, ncq):
    n = aq.shape[0]
    nqc, nql = tc // tq, t // tq

    def qidx(bi, j):
        return (jnp.where(j < nqc, bi * nqc + j, ncq + bi * nql + (j - nqc)), 0)

    return pl.pallas_call(
        functools.partial(_attn_kernel, tq=tq, tc=tc, t=t, tk=tk, nqc=nqc),
        grid=(b, nqc + nql),
        in_specs=[pl.BlockSpec((tq, 512), qidx),
                  pl.BlockSpec((1, tc + t, 256), lambda bi, j: (bi, 0, 0)),
                  pl.BlockSpec((1, tc + t, 512), lambda bi, j: (bi, 0, 0))],
        out_specs=pl.BlockSpec((tq, 512), qidx),
        out_shape=jax.ShapeDtypeStruct((n, 512), F32),
        scratch_shapes=[pltpu.VMEM((ATTN_KV_HEADS, 4 * tq, LANES), BF16),
                        pltpu.VMEM((ATTN_KV_HEADS, 4 * tq, LANES), F32),
                        pltpu.VMEM((ATTN_KV_HEADS, 2 * tq, LANES), F32),
                        pltpu.VMEM((ATTN_KV_HEADS, 2 * tq, LANES), F32)],
        compiler_params=_cparams(2, 48),
        name="attn",
    )(aq, kd, vd)


def _merge_kernel(hf_ref, hb_ref, mo_ref, gm_ref, a_ref, x_ref, mod_ref, mn_ref, wo_ref, rwh_ref, rwl_ref, rb_ref,
                  xo_ref, h2_ref, lg_ref):
    ind = _seg_indicator(2 * LANES)
    parts = [_sigmoid(mo_ref[...]) * (hf_ref[...] + hb_ref[...]), gm_ref[...],
             a_ref[:, 0:256], a_ref[:, 256:512]]
    y = jnp.concatenate([_head_rms(pt, ind) for pt in parts], axis=1) * mn_ref[...]
    xn = x_ref[...] + mod_ref[0, 2:3, :] * _dot(y.astype(BF16), wo_ref[...])
    xo_ref[...] = xn
    h2 = _rms(xn) * (1.0 + mod_ref[0, 4:5, :]) + mod_ref[0, 3:4, :]
    nsl = h2.shape[1] // LANES
    for sl in range(nsl):
        h2_ref[pl.ds(sl, h2.shape[0], stride=nsl), :] = h2[:, sl * LANES:(sl + 1) * LANES]
    hi, lo = _split(h2)
    lg_ref[...] = _dot(hi, rwh_ref[...]) + _dot(lo, rwh_ref[...]) + _dot(hi, rwl_ref[...]) + rb_ref[...]


def _merge_call(hf, hb, mo, gm, a, x, mods, mn, wo, rwh, rwl, rb, *, l, tm, nct, tpb):
    n, d = x.shape

    def mod_idx(i):
        return (l, jnp.where(i < nct, 0, 1 + (i - nct) // tpb), 0, 0)

    row = lambda i: (i, 0)
    lay = lambda i: (l, 0, 0)
    return pl.pallas_call(
        _merge_kernel,
        grid=(n // tm,),
        in_specs=[pl.BlockSpec((tm, 256), row), pl.BlockSpec((tm, 256), row), pl.BlockSpec((tm, 256), row),
                  pl.BlockSpec((tm, 256), row), pl.BlockSpec((tm, 512), row), pl.BlockSpec((tm, d), row),
                  pl.BlockSpec((None, 1, 6, d), mod_idx),
                  pl.BlockSpec((None, 1, d), lay),
                  pl.BlockSpec((None, d, d), lay),
                  pl.BlockSpec((None, d, LANES), lay),
                  pl.BlockSpec((None, d, LANES), lay),
                  pl.BlockSpec((None, 1, LANES), lay)],
        out_specs=[pl.BlockSpec((tm, d), row), pl.BlockSpec((tm * (d // LANES), LANES), row),
                   pl.BlockSpec((tm, LANES), row)],
        out_shape=[jax.ShapeDtypeStruct((n, d), F32), jax.ShapeDtypeStruct((n * (d // LANES), LANES), F32),
                   jax.ShapeDtypeStruct((n, LANES), F32)],
        compiler_params=_cparams(1, 48),
        name="merge",
    )(hf, hb, mo, gm, a, x, mods, mn, wo, rwh, rwl, rb)


def _wprep_kernel(w_ref, o_ref, *, f):
    slab = 2 * LANES
    r = lax.broadcasted_iota(jnp.int32, (slab, slab), 0)
    c = lax.broadcasted_iota(jnp.int32, (slab, slab), 1)
    perm = jnp.where(r == jnp.where(c < LANES, 2 * c, 2 * (c - LANES) + 1), 1.0, 0.0).astype(BF16)
    for s in range(2 * f // slab):
        y = _dot(w_ref[:, s * slab:(s + 1) * slab].astype(BF16), perm).astype(BF16)
        o_ref[:, s * LANES:(s + 1) * LANES] = y[:, :LANES]
        o_ref[:, f + s * LANES:f + (s + 1) * LANES] = y[:, LANES:]


def _wprep_call(w_gate_up):
    depth, ne, d, f2 = w_gate_up.shape
    rows = 512
    w2 = w_gate_up.reshape(depth * ne * d, f2)
    out = pl.pallas_call(
        functools.partial(_wprep_kernel, f=f2 // 2),
        grid=(depth * ne * d // rows,),
        in_specs=[pl.BlockSpec((rows, f2), lambda i: (i, 0))],
        out_specs=pl.BlockSpec((rows, f2), lambda i: (i, 0)),
        out_shape=jax.ShapeDtypeStruct((depth * ne * d, f2), BF16),
        compiler_params=_cparams(1, 32),
        name="wprep",
    )(w2)
    return out.reshape(depth, ne, d, f2)


def _moe_kernel(be_ref, nv_ref, tok0_ref, tokn_ref, h_hbm, wgl_ref, wd_ref, bgl_ref, bd_ref, y_ref,
                xbuf, act_ref, sem, *, tmb, f):
    i = pl.program_id(0)
    nv = nv_ref[0]
    slot = i % 2
    cw = 2 * LANES
    nchunk = f // cw
    per = tmb // (2 * nchunk)
    nsl = h_hbm.shape[1]

    def request(tok_ref, r, dst, priority=0):
        pltpu.make_async_copy(h_hbm.at[tok_ref[0, 0, r]], xbuf.at[dst, pl.ds(pl.multiple_of(r * nsl, nsl), nsl)],
                              sem.at[dst]).start(priority=priority)

    @pl.when(i == 0)
    def _():
        def issue(r, carry):
            request(tok0_ref, r, 0)
            return carry
        lax.fori_loop(0, tmb, issue, 0, unroll=8)

    @pl.when(i <= nv)
    def _():
        pltpu.make_async_copy(xbuf.at[slot], xbuf.at[slot], sem.at[slot]).wait()

    def compute(cur):
        xb = jnp.concatenate([xbuf[cur, pl.ds(sl, tmb, stride=nsl), :] for sl in range(nsl)], axis=1).astype(BF16)
        for j in range(nchunk):
            cols = slice(j * cw, (j + 1) * cw)
            ucols = slice(f + j * cw, f + (j + 1) * cw)
            x_glu = jnp.minimum(_dot(xb, wgl_ref[:, cols]) + bgl_ref[:, cols], SWIGLU_LIMIT)
            for r in range(2 * j * per, (2 * j + 1) * per):
                request(tokn_ref, r, 1 - cur, r % 2)
            x_lin = jnp.clip(_dot(xb, wgl_ref[:, ucols]) + bgl_ref[:, ucols], -SWIGLU_LIMIT, SWIGLU_LIMIT)
            for r in range((2 * j + 1) * per, (2 * j + 2) * per):
                request(tokn_ref, r, 1 - cur, r % 2)
            act_ref[:, cols] = (x_glu * _sigmoid(SWIGLU_ALPHA * x_glu) * (x_lin + 1.0)).astype(BF16)
        act = act_ref[...]
        for j in range(wd_ref.shape[1] // cw):
            cols = slice(j * cw, (j + 1) * cw)
            y = _dot(act, wd_ref[:, cols]) + bd_ref[:, cols]
            for sl in range(cw // LANES):
                y_ref[pl.ds(j * (cw // LANES) + sl, tmb, stride=nsl), :] = y[:, sl * LANES:(sl + 1) * LANES]

    for cur in range(2):
        pl.when(jnp.logical_and(i < nv, slot == cur))(functools.partial(compute, cur))

    @pl.when(i >= nv)
    def _():
        y_ref[...] = jnp.zeros_like(y_ref)


def _moe_call(block_expert, nvalid, slot_tok, h2, wgl, wd, bgl, bd, *, l, tmb):
    d = wd.shape[-1]
    nb = block_expert.shape[0]
    f = wd.shape[-2]
    wmap = lambda i, be, nv: (l, be[i], 0, 0)
    grid_spec = pltpu.PrefetchScalarGridSpec(
        num_scalar_prefetch=2,
        grid=(nb,),
        in_specs=[pl.BlockSpec((1, 1, tmb), lambda i, be, nv: (0, 0, 0), memory_space=pltpu.SMEM),
                  pl.BlockSpec((1, 1, tmb), lambda i, be, nv: (jnp.minimum(i + 1, nb - 1), 0, 0),
                               memory_space=pltpu.SMEM),
                  pl.BlockSpec(memory_space=pl.ANY),
                  pl.BlockSpec((None, None, d, 2 * f), wmap),
                  pl.BlockSpec((None, None, f, d), wmap),
                  pl.BlockSpec((None, None, 1, 2 * f), wmap),
                  pl.BlockSpec((None, None, 1, d), wmap)],
        out_specs=pl.BlockSpec((tmb * (d // LANES), LANES), lambda i, be, nv: (i, 0)),
        scratch_shapes=[pltpu.VMEM((2, tmb * (d // LANES), LANES), F32), pltpu.VMEM((tmb, f), BF16),
                        pltpu.SemaphoreType.DMA((2,))],
    )
    return pl.pallas_call(
        functools.partial(_moe_kernel, tmb=tmb, f=f),
        grid_spec=grid_spec,
        out_shape=jax.ShapeDtypeStruct((nb * tmb * (d // LANES), LANES), F32),
        compiler_params=_cparams(1, 48),
        name="moe",
    )(block_expert, nvalid, slot_tok, slot_tok, h2, wgl, wd, bgl, bd)


def _combine_kernel(dest0_ref, destn_ref, y_hbm, gate_ref, x_ref, mod_ref, o_ref, buf, sem, *, tt):
    i = pl.program_id(0)
    slot = i % 2
    nsl = y_hbm.shape[1]

    def request_tile(dest_ref, dst):
        def issue(r, carry):
            for k in range(TOP_K):
                pltpu.make_async_copy(y_hbm.at[dest_ref[0, 0, k * tt + r]],
                                      buf.at[dst, k, pl.ds(pl.multiple_of(r * nsl, nsl), nsl)],
                                      sem.at[dst]).start(priority=k % 2)
            return carry
        lax.fori_loop(0, tt, issue, 0, unroll=4)

    @pl.when(i == 0)
    def _():
        request_tile(dest0_ref, 0)

    @pl.when(i + 1 < pl.num_programs(0))
    def _():
        request_tile(destn_ref, 1 - slot)

    for k in range(TOP_K):
        pltpu.make_async_copy(buf.at[slot, k], buf.at[slot, k], sem.at[slot]).wait()
    gate = gate_ref[...]
    f = None
    for k in range(TOP_K):
        rows = jnp.concatenate([buf[slot, k, pl.ds(sl, tt, stride=nsl), :] for sl in range(nsl)], axis=1)
        g = gate[:, 2 * TOP_K + k:2 * TOP_K + k + 1]
        f = g * rows if f is None else f + g * rows
    o_ref[...] = x_ref[...] + mod_ref[0, 5:6, :] * f


def _combine_call(dest, y_slots, gates, x, mods, *, l, tt, nct, tpb):
    n, d = x.shape
    ntile = n // tt

    def mod_idx(i):
        return (l, jnp.where(i < nct, 0, 1 + (i - nct) // tpb), 0, 0)

    row = lambda i: (i, 0)
    return pl.pallas_call(
        functools.partial(_combine_kernel, tt=tt),
        grid=(ntile,),
        in_specs=[pl.BlockSpec((1, 1, tt * TOP_K), lambda i: (0, 0, 0), memory_space=pltpu.SMEM),
                  pl.BlockSpec((1, 1, tt * TOP_K), lambda i: (jnp.minimum(i + 1, ntile - 1), 0, 0),
                               memory_space=pltpu.SMEM),
                  pl.BlockSpec(memory_space=pl.ANY),
                  pl.BlockSpec((tt, LANES), row),
                  pl.BlockSpec((tt, d), row),
                  pl.BlockSpec((None, 1, 6, d), mod_idx)],
        out_specs=pl.BlockSpec((tt, d), row),
        out_shape=jax.ShapeDtypeStruct((n, d), F32),
        scratch_shapes=[pltpu.VMEM((2, TOP_K, tt * (d // LANES), LANES), F32), pltpu.SemaphoreType.DMA((2,))],
        compiler_params=_cparams(1, 32),
        name="combine",
    )(dest, dest, y_slots, gates, x, mods)


def _final_kernel(x_ref, g_ref, o_ref):
    o_ref[...] = _rms(x_ref[...]) * g_ref[...]


def _final_call(x, gain, *, tm, nct, nl):
    d = x.shape[1]
    return pl.pallas_call(
        _final_kernel,
        grid=(nl // tm,),
        in_specs=[pl.BlockSpec((tm, d), lambda i: (nct + i, 0)), pl.BlockSpec((1, d), lambda i: (0, 0))],
        out_specs=pl.BlockSpec((tm, d), lambda i: (i, 0)),
        out_shape=jax.ShapeDtypeStruct((nl, d), F32),
        compiler_params=_cparams(1, 32),
        name="final_norm",
    )(x, gain.reshape(1, d))


def _route_kernel(lg_ref, rn_ref, rt_ref, cnt_ref, run_ref, *, tr):
    @pl.when(pl.program_id(0) == 0)
    def _():
        run_ref[...] = jnp.zeros_like(run_ref)

    lane = lax.broadcasted_iota(jnp.int32, (tr, LANES), 1).astype(F32)
    lg = jnp.where(lane < N_EXPERTS, lg_ref[...], -jnp.inf)
    onehot = jnp.zeros((tr, LANES), F32)
    vals, idxs = [], []
    for _ in range(TOP_K):
        m = jnp.max(lg, axis=-1, keepdims=True)
        idx = jnp.min(jnp.where(lg == m, lane, float(LANES)), axis=-1, keepdims=True)
        hit = lane == idx
        onehot = jnp.where(hit, 1.0, onehot)
        lg = jnp.where(hit, -jnp.inf, lg)
        vals.append(m)
        idxs.append(idx)
    es = [jnp.exp(v - vals[0]) for v in vals]
    tot = es[0] + es[1] + es[2] + es[3]
    gates = [e / tot for e in es]
    r = lax.broadcasted_iota(jnp.int32, (tr, tr), 0)
    c = lax.broadcasted_iota(jnp.int32, (tr, tr), 1)
    strict = jnp.where(c < r, 1.0, 0.0).astype(BF16)
    run = run_ref[0:1, :]
    before = run + _dot(strict, onehot.astype(BF16))
    ranks = [jnp.sum(jnp.where(lane == idx, before, 0.0), axis=-1, keepdims=True) for idx in idxs]
    run_ref[...] = jnp.broadcast_to(run + jnp.sum(onehot, axis=0, keepdims=True), run_ref.shape)
    cnt_ref[...] = run_ref[...]
    out = jnp.zeros((tr, LANES), F32)
    for j, col in enumerate(idxs + ranks + gates):
        out = jnp.where(lane == float(j), col, out)
    rn_ref[...] = out
    rt_ref[...] = out.T


def _route_call(logits, *, tr):
    n = logits.shape[0]
    return pl.pallas_call(
        functools.partial(_route_kernel, tr=tr),
        grid=(n // tr,),
        in_specs=[pl.BlockSpec((tr, LANES), lambda i: (i, 0))],
        out_specs=[pl.BlockSpec((tr, LANES), lambda i: (i, 0)), pl.BlockSpec((LANES, tr), lambda i: (0, i)),
                   pl.BlockSpec((8, LANES), lambda i: (0, 0))],
        out_shape=[jax.ShapeDtypeStruct((n, LANES), F32), jax.ShapeDtypeStruct((LANES, n), F32),
                   jax.ShapeDtypeStruct((8, LANES), F32)],
        scratch_shapes=[pltpu.VMEM((8, LANES), F32)],
        compiler_params=_cparams(1, 32),
        name="route",
    )(logits)


def _slots(rt, cnt, tmb, n_blocks):
    n = rt.shape[1]
    counts = cnt[0, :N_EXPERTS].astype(jnp.int32)
    padded = (counts + tmb - 1) // tmb * tmb
    ends = jnp.cumsum(padded)
    start = ends - padded
    idx_t = rt[0:TOP_K].astype(jnp.int32)
    start_sel = jnp.sum(jnp.where(idx_t[..., None] == jnp.arange(N_EXPERTS, dtype=jnp.int32), start, 0), axis=-1)
    dest_t = start_sel + rt[TOP_K:2 * TOP_K].astype(jnp.int32)
    nvalid = (ends[-1] // tmb).astype(jnp.int32)
    blk = jnp.minimum(jnp.arange(n_blocks, dtype=jnp.int32), nvalid - 1)
    block_expert = jnp.minimum(jnp.searchsorted(ends, blk * tmb, side='right'), N_EXPERTS - 1).astype(jnp.int32)
    nfill = n_blocks * tmb - n * TOP_K
    pad = padded - counts
    padcum = jnp.cumsum(pad)
    j = jnp.arange(nfill, dtype=jnp.int32)
    fe = jnp.sum(j[:, None] >= padcum[None, :], axis=1)
    first_free = jnp.concatenate([ends - pad, ends[-1:]])
    fill_slot = first_free[fe] + j - jnp.concatenate([padcum - pad, padcum[-1:]])[fe]
    tok = jnp.broadcast_to(jnp.arange(n, dtype=jnp.int32)[None, :], (TOP_K, n)).reshape(-1)
    _, slot_tok = lax.sort((jnp.concatenate([dest_t.reshape(-1), fill_slot]).astype(jnp.int32),
                            jnp.concatenate([tok, jnp.zeros((nfill,), jnp.int32)])), num_keys=1)
    return dest_t, slot_tok, block_expert, nvalid.reshape(1)


def _rope_tables(t, tm):
    rows = t // GRID_W
    row = jnp.repeat(jnp.arange(rows, dtype=F32), GRID_W)
    col = jnp.tile(jnp.arange(GRID_W, dtype=F32), rows)
    npair = HEAD_DIM // 4
    inv_freq = ROPE_THETA ** (-jnp.arange(npair, dtype=F32) / npair)
    ang = jnp.concatenate([row[:, None] * inv_freq, col[:, None] * inv_freq], axis=-1)
    cos = jnp.tile(jnp.cos(ang), (1, 4))
    sin = jnp.tile(jnp.concatenate([-jnp.sin(ang), jnp.sin(ang)], axis=-1), (1, 2))
    cos = jnp.concatenate([jnp.ones((tm, LANES), F32), cos], axis=0)
    sin = jnp.concatenate([jnp.zeros((tm, LANES), F32), sin], axis=0)
    return cos, sin


def _tile(nc, t, cap):
    for cand in (1024, 512, 256, 128):
        if cand <= cap and nc % cand == 0 and t % cand == 0:
            return cand
    raise ValueError("unsupported sequence lengths")


def kernel(x, c, ctx, c_ctx, w_ada, b_ada, w_in, b_in, q_norm, k_norm, gmlp_norm, gmlp_ws, gmlp_b, mix_norm, w_out,
           router_w, router_b, w_gate_up, b_gate_up, w_down, b_down, final_norm):
    b, t, d = x.shape
    tc = ctx.shape[1]
    depth = w_in.shape[0]
    nc, nl = b * tc, b * t
    n = nc + nl
    tm = _tile(nc, t, 512)
    rs = _tile(tc, t, 256)
    tq = _tile(tc, t, 256)
    tk = _tile(t, t, 512)
    tmb = 512
    tt = _tile(nc, t, 256)
    nct, tpb = nc // tm, t // tm
    n_blocks = -(-n * TOP_K // tmb) + N_EXPERTS

    def heads_0213(a, axis):
        blk = [lax.slice_in_dim(a, i * HEAD_DIM, (i + 1) * HEAD_DIM, axis=axis) for i in (0, 2, 1, 3)]
        rest = lax.slice_in_dim(a, MLSTM_HEADS * HEAD_DIM, a.shape[axis], axis=axis)
        return jnp.concatenate(blk + [rest], axis=axis)

    def cols(a):
        pad = jnp.zeros(a.shape[:-1] + (LANES - 16,), a.dtype)
        return jnp.concatenate([a[..., 0:768], heads_0213(a[..., 768:1024], a.ndim - 1), a[..., 1040:2320],
                                a[..., 1024:1040], pad], axis=-1)

    w_in_r = cols(w_in).astype(BF16)
    b_in_r = cols(b_in).reshape(depth, 1, C_END)
    qn = jnp.tile(q_norm, (1, ATTN_HEADS)).reshape(depth, 1, 512)
    kn = jnp.tile(k_norm, (1, 2)).reshape(depth, 1, LANES)
    gn = gmlp_norm.reshape(depth, 1, 256)
    ws4 = gmlp_ws.reshape(depth, GMLP_GROUPS * GMLP_CHUNK, GMLP_CHUNK).astype(BF16)
    gb = jnp.repeat(jnp.swapaxes(gmlp_b, 1, 2), HEAD_DIM, axis=2)
    mn = heads_0213(mix_norm, 1).reshape(depth, 1, d)
    wo = heads_0213(w_out, 1).astype(BF16)
    rw = jnp.pad(router_w, ((0, 0), (0, 0), (0, LANES - N_EXPERTS)))
    rwh = rw.astype(BF16)
    rwl = (rw - rwh.astype(F32)).astype(BF16)
    rb = jnp.pad(router_b, ((0, 0), (0, LANES - N_EXPERTS))).reshape(depth, 1, LANES)
    wgl = _wprep_call(w_gate_up)
    wd = w_down.astype(BF16)
    ne, ff = w_down.shape[1], w_down.shape[2]
    bgl = jnp.concatenate([b_gate_up[..., 0::2], b_gate_up[..., 1::2]], axis=-1).reshape(depth, ne, 1, 2 * ff)
    bd = b_down.reshape(depth, ne, 1, d)
    cos_t, sin_t = _rope_tables(t, tm)

    r = -(-(b + 1) // 8) * 8
    cs = jnp.zeros((r, d), F32).at[0].set(c_ctx).at[1:b + 1].set(c)
    mods = _ada_call(cs, w_ada, b_ada).reshape(depth, r, 6, d)

    xs = jnp.concatenate([ctx.reshape(nc, d), x.reshape(nl, d)], axis=0)
    for l in range(depth):
        qtd, mk, vt2, mo, g, gm, aq, akd, avd = _proj_call(
            xs, mods, w_in_r, b_in_r, qn, kn, gn, ws4, gb, cos_t, sin_t, l=l, tm=tm, nct=nct, tpb=tpb)
        gt3 = _mlstm_gate_rows(g)
        hf, hb = _mlstm_call(mk, qtd, vt2, g, gt3, b=b, rs=rs, nsc=tc // rs, nsl=t // rs, ncb=nc // rs)
        kd = jnp.concatenate([akd[:nc].reshape(b, tc, 256), akd[nc:].reshape(b, t, 256)], axis=1)
        vd = jnp.concatenate([avd[:nc].reshape(b, tc, 512), avd[nc:].reshape(b, t, 512)], axis=1)
        a = _attn_call(aq, kd, vd, b=b, tq=tq, tc=tc, t=t, tk=tk, ncq=nc // tq)
        xs, h2, logits = _merge_call(hf, hb, mo, gm, a, xs, mods, mn, wo, rwh, rwl, rb, l=l, tm=tm, nct=nct, tpb=tpb)
        rn, rt, cnt = _route_call(logits, tr=tm)
        dest_t, slot_tok, block_expert, nvalid = _slots(rt, cnt, tmb, n_blocks)
        y_slots = _moe_call(block_expert, nvalid, slot_tok.reshape(n_blocks, 1, tmb),
                            h2.reshape(n, d // LANES, LANES), wgl, wd, bgl, bd, l=l, tmb=tmb)
        dest_blk = dest_t.reshape(TOP_K, n // tt, tt).transpose(1, 0, 2).reshape(n // tt, 1, TOP_K * tt)
        xs = _combine_call(dest_blk, y_slots.reshape(n_blocks * tmb, d // LANES, LANES), rn, xs, mods, l=l, tt=tt,
                           nct=nc // tt, tpb=t // tt)
    out = _final_call(xs, final_norm, tm=tm, nct=nct, nl=nl)
    return out.reshape(b, t, d)
```

```python
import functools

import jax
import jax.numpy as jnp
from jax import lax
from jax.experimental import pallas as pl
from jax.experimental.pallas import tpu as pltpu

F32 = jnp.float32
BF16 = jnp.bfloat16

HEAD_DIM = 64
EPS = 1e-6
GRID_W = 64
ROPE_THETA = 10000.0
MLSTM_HEADS = 4
MLSTM_CHUNK = 64
GMLP_GROUPS = 4
GMLP_CHUNK = 128
ATTN_HEADS = 8
ATTN_KV_HEADS = 2
N_EXPERTS = 32
TOP_K = 4
SWIGLU_LIMIT = 7.0
SWIGLU_ALPHA = 1.702
LANES = 128
NEG_BIG = -1e30

C_MQ, C_MK, C_MV, C_MO, C_GU, C_GV, C_AQ, C_AK, C_AV, C_G, C_END = (
    0, 256, 512, 768, 1024, 1280, 1536, 2048, 2176, 2304, 2432)


def _dot(a, b):
    return jnp.dot(a, b, preferred_element_type=F32)


def _dot_nt(a, b):
    return lax.dot_general(a, b, (((1,), (1,)), ((), ())), preferred_element_type=F32)


def _split(x):
    hi = x.astype(BF16)
    lo = (x - hi.astype(F32)).astype(BF16)
    return hi, lo


def _cparams(ndims, vmem_mb):
    return pltpu.CompilerParams(dimension_semantics=("arbitrary",) * ndims,
                                vmem_limit_bytes=vmem_mb << 20)


def _sigmoid(x):
    return 1.0 / (1.0 + jnp.exp(-x))


def _gelu_tanh(x):
    return 0.5 * x * (1.0 + jnp.tanh(0.7978845608028654 * (x + 0.044715 * (x * x * x))))


def _rms(x):
    return x * lax.rsqrt(jnp.mean(x * x, axis=-1, keepdims=True) + EPS)


def _seg_indicator(n):
    r = lax.broadcasted_iota(jnp.int32, (n, n), 0) // HEAD_DIM
    c = lax.broadcasted_iota(jnp.int32, (n, n), 1) // HEAD_DIM
    return jnp.where(r == c, 1.0, 0.0).astype(BF16)


def _head_rms(x, ind):
    hi, lo = _split(x * x)
    ss = _dot(hi, ind) + _dot(lo, ind)
    return x * lax.rsqrt(ss * (1.0 / HEAD_DIM) + EPS)


def _ada_kernel(c_ref, w_ref, b_ref, o_ref):
    c = c_ref[...]
    s = c * _sigmoid(c)
    s_hi, s_lo = _split(s)
    w_hi, w_lo = _split(w_ref[0])
    o_ref[0] = _dot(s_hi, w_hi) + _dot(s_lo, w_hi) + _dot(s_hi, w_lo) + b_ref[0]


def _ada_call(cs, w_ada, b_ada):
    depth, d, d6 = w_ada.shape
    r = cs.shape[0]
    tn = 1536
    return pl.pallas_call(
        _ada_kernel,
        grid=(depth, d6 // tn),
        in_specs=[pl.BlockSpec((r, d), lambda l, j: (0, 0)),
                  pl.BlockSpec((1, d, tn), lambda l, j: (l, 0, j)),
                  pl.BlockSpec((1, 1, tn), lambda l, j: (l, 0, j))],
        out_specs=pl.BlockSpec((1, r, tn), lambda l, j: (l, 0, j)),
        out_shape=jax.ShapeDtypeStruct((depth, r, d6), F32),
        compiler_params=_cparams(2, 40),
        name="ada",
    )(cs, w_ada, b_ada.reshape(depth, 1, d6))


def _proj_kernel(x_ref, mod_ref, w_ref, b_ref, qn_ref, kn_ref, gn_ref, ws_ref, gb_ref, cos_ref, sin_ref,
                 qt_ref, mk_ref, vt_ref, mo_ref, g_ref, gm_ref, aq_ref, ak_ref, av_ref, *, tm):
    h = (_rms(x_ref[...]) * (1.0 + mod_ref[0, 1:2, :]) + mod_ref[0, 0:1, :]).astype(BF16)

    def seg(a, b):
        return _dot(h, w_ref[:, a:b]) + b_ref[:, a:b]

    lo_q = lax.broadcasted_iota(jnp.int32, (2 * LANES, LANES), 1) < MLSTM_CHUNK
    lo_v = lax.broadcasted_iota(jnp.int32, (HEAD_DIM, LANES), 1) < MLSTM_CHUNK
    qt = seg(C_MQ, C_MK).T
    vt = seg(C_MV, C_MO).T
    for m in range(tm // LANES):
        even, odd = slice(2 * m * LANES, (2 * m + 1) * LANES), slice((2 * m + 1) * LANES, (2 * m + 2) * LANES)
        qv = qt[:, m * LANES:(m + 1) * LANES]
        qr = pltpu.roll(qv, MLSTM_CHUNK, 1)
        qt_ref[:, even] = jnp.where(lo_q, qv, qr).astype(BF16)
        qt_ref[:, odd] = jnp.where(lo_q, qr, qv).astype(BF16)
        for p in range(MLSTM_HEADS // 2):
            va = vt[p * LANES:p * LANES + HEAD_DIM, m * LANES:(m + 1) * LANES]
            vb = vt[p * LANES + HEAD_DIM:(p + 1) * LANES, m * LANES:(m + 1) * LANES]
            vt_ref[p, :, even] = jnp.where(lo_v, va, pltpu.roll(vb, MLSTM_CHUNK, 1)).astype(BF16)
            vt_ref[p, :, odd] = jnp.where(lo_v, pltpu.roll(va, MLSTM_CHUNK, 1), vb).astype(BF16)
    mk_ref[...] = (seg(C_MK, C_MV) * HEAD_DIM ** -0.5).astype(BF16)
    mo_ref[...] = seg(C_MO, C_GU)

    g = seg(C_G, C_END)
    ls = jnp.minimum(g, 0.0) - jnp.log1p(jnp.exp(-jnp.abs(g)))
    r = lax.broadcasted_iota(jnp.int32, (tm, tm), 0)
    c = lax.broadcasted_iota(jnp.int32, (tm, tm), 1)
    same = (r // MLSTM_CHUNK) == (c // MLSTM_CHUNK)
    pre = jnp.where(same, jnp.where(c <= r, 1.0, 0.0), 0.0).astype(BF16)
    suf = jnp.where(same, jnp.where(c >= r, 1.0, 0.0), 0.0).astype(BF16)
    ls_hi, ls_lo = _split(ls)
    psum = _dot(pre, ls_hi) + _dot(pre, ls_lo)
    ssum = _dot(suf, ls_hi) + _dot(suf, ls_lo)
    lane = lax.broadcasted_iota(jnp.int32, (tm, LANES), 1)
    q4 = lane // MLSTM_HEADS
    g_ref[...] = jnp.where(q4 == 1, psum, jnp.where(q4 == 3, ssum, g))

    u = _gelu_tanh(seg(C_GU, C_GV))
    v = (_rms(_gelu_tanh(seg(C_GV, C_AQ))) * gn_ref[...]).astype(BF16)
    grp = lax.broadcasted_iota(jnp.int32, (GMLP_CHUNK, 2 * LANES), 1) // HEAD_DIM
    for ci in range(tm // GMLP_CHUNK):
        rows = slice(ci * GMLP_CHUNK, (ci + 1) * GMLP_CHUNK)
        full = _dot(ws_ref[...], v[rows, :])
        mixed = gb_ref[...]
        for gi in range(GMLP_GROUPS):
            mixed = mixed + jnp.where(grp == gi, full[gi * GMLP_CHUNK:(gi + 1) * GMLP_CHUNK, :], 0.0)
        gm_ref[rows, :] = u[rows, :] * mixed

    cos = cos_ref[...]
    sin = sin_ref[...]
    first_half = (lane % HEAD_DIM) < (HEAD_DIM // 2)
    lo_half = lane < HEAD_DIM

    def rope(blk):
        partner = jnp.where(first_half, pltpu.roll(blk, LANES - HEAD_DIM // 2, 1),
                            pltpu.roll(blk, HEAD_DIM // 2, 1))
        return blk * cos + partner * sin

    ind256 = _seg_indicator(2 * LANES)
    for s in range(2):
        qn = _head_rms(seg(C_AQ + s * 256, C_AQ + (s + 1) * 256), ind256) * qn_ref[:, s * 256:(s + 1) * 256]
        for j in range(2):
            blk = rope(qn[:, j * LANES:(j + 1) * LANES]) * HEAD_DIM ** -0.5
            aq_ref[:, s * 256 + j * LANES: s * 256 + (j + 1) * LANES] = blk.astype(BF16)

    kr = rope(_head_rms(seg(C_AK, C_AV), ind256[:LANES, :LANES]) * kn_ref[...])
    kroll = pltpu.roll(kr, HEAD_DIM, 1)
    ak_ref[:, 0:LANES] = jnp.where(lo_half, kr, kroll).astype(BF16)
    ak_ref[:, LANES:2 * LANES] = jnp.where(lo_half, kroll, kr).astype(BF16)

    vf = seg(C_AV, C_G)
    vroll = pltpu.roll(vf, HEAD_DIM, 1)
    av_ref[:, 0:LANES] = jnp.where(lo_half, vf, 1.0).astype(BF16)
    av_ref[:, LANES:2 * LANES] = jnp.where(lo_half, 1.0, vroll).astype(BF16)
    av_ref[:, 2 * LANES:3 * LANES] = jnp.where(lo_half, vroll, 1.0).astype(BF16)
    av_ref[:, 3 * LANES:4 * LANES] = jnp.where(lo_half, 1.0, vf).astype(BF16)


def _proj_call(x, mods, w, b, qn, kn, gn, ws4, gb, cos_t, sin_t, *, l, tm, nct, tpb):
    n, d = x.shape

    def mod_idx(i):
        return (l, jnp.where(i < nct, 0, 1 + (i - nct) // tpb), 0, 0)

    def rope_idx(i):
        return (jnp.where(i < nct, 0, 1 + (i - nct) % tpb), 0)

    row = lambda i: (i, 0)
    const2 = lambda i: (0, 0)
    lay = lambda i: (l, 0, 0)
    nlane = tm // MLSTM_CHUNK * LANES
    outs = [(256, BF16), (256, F32), (LANES, F32), (256, F32), (512, BF16), (256, BF16), (512, BF16)]
    row_specs = [pl.BlockSpec((tm, wd), row) for wd, _ in outs]
    row_shapes = [jax.ShapeDtypeStruct((n, wd), dt) for wd, dt in outs]
    qt_spec = pl.BlockSpec((256, nlane), lambda i: (0, i))
    vt_spec = pl.BlockSpec((MLSTM_HEADS // 2, HEAD_DIM, nlane), lambda i: (0, 0, i))
    qt_shape = jax.ShapeDtypeStruct((256, n // MLSTM_CHUNK * LANES), BF16)
    vt_shape = jax.ShapeDtypeStruct((MLSTM_HEADS // 2, HEAD_DIM, n // MLSTM_CHUNK * LANES), BF16)
    return pl.pallas_call(
        functools.partial(_proj_kernel, tm=tm),
        grid=(n // tm,),
        in_specs=[pl.BlockSpec((tm, d), row),
                  pl.BlockSpec((None, 1, 6, d), mod_idx),
                  pl.BlockSpec((None, d, C_END), lay),
                  pl.BlockSpec((None, 1, C_END), lay),
                  pl.BlockSpec((None, 1, 512), lay),
                  pl.BlockSpec((None, 1, LANES), lay),
                  pl.BlockSpec((None, 1, 256), lay),
                  pl.BlockSpec((None, GMLP_GROUPS * GMLP_CHUNK, GMLP_CHUNK), lay),
                  pl.BlockSpec((None, GMLP_CHUNK, 256), lay),
                  pl.BlockSpec((tm, LANES), rope_idx),
                  pl.BlockSpec((tm, LANES), rope_idx)],
        out_specs=[qt_spec, row_specs[0], vt_spec] + row_specs[1:],
        out_shape=[qt_shape, row_shapes[0], vt_shape] + row_shapes[1:],
        compiler_params=_cparams(1, 48),
        name="proj",
    )(x, mods, w, b, qn, kn, gn, ws4, gb, cos_t, sin_t)


def _mlstm_kernel(kf_ref, qf_ref, vf_ref, gf_ref, gtf_ref, kb_ref, qb_ref, vb_ref, gb_ref, gtb_ref,
                  hf_ref, hb_ref, c_ref, m_ref, *, nch):
    ch = MLSTM_CHUNK

    @pl.when(pl.program_id(1) == 0)
    def _():
        c_ref[...] = jnp.zeros_like(c_ref)
        m_ref[...] = jnp.zeros_like(m_ref)

    sub = lax.broadcasted_iota(jnp.int32, (ch, LANES), 0)
    lane = lax.broadcasted_iota(jnp.int32, (ch, LANES), 1)
    lo_half = lane < ch
    pos = lane % ch
    r2 = lax.broadcasted_iota(jnp.int32, (LANES, LANES), 0)
    c2 = lax.broadcasted_iota(jnp.int32, (LANES, LANES), 1)
    blockdiag = (r2 < ch) == (c2 < ch)
    ones_rows = jnp.ones((ch, LANES), BF16)
    dirs = ((kf_ref, qf_ref, vf_ref, gf_ref, gtf_ref, hf_ref), (kb_ref, qb_ref, vb_ref, gb_ref, gtb_ref, hb_ref))
    for d, (k_ref, q_ref, v_ref, g_ref, gt_ref, h_ref) in enumerate(dirs):
        mask = (sub <= pos) if d == 0 else (sub >= pos)
        sels = []
        for p in range(MLSTM_HEADS // 2):
            gi = d * 2 * MLSTM_HEADS + 2 * p + jnp.where(c2 >= ch, 1, 0)
            sels.append((jnp.where(r2 == gi, 1.0, 0.0) - jnp.where(r2 == gi + MLSTM_HEADS, 1.0, 0.0)).astype(BF16))
        order = range(nch) if d == 0 else range(nch - 1, -1, -1)
        for ci in order:
            rows = slice(ci * ch, (ci + 1) * ch)
            cl = slice(ci * LANES, (ci + 1) * LANES)
            g = g_ref[rows, :]
            g_hi = g.astype(BF16)
            g_r = g - g_hi.astype(F32)
            g_mid = g_r.astype(BF16)
            g_lo = (g_r - g_mid.astype(F32)).astype(BF16)
            hts = []
            for p in range(MLSTM_HEADS // 2):
                st = d * (MLSTM_HEADS // 2) + p
                kp = k_ref[rows, p * LANES:(p + 1) * LANES]
                zero = jnp.zeros_like(kp)
                qbd = jnp.where(blockdiag, q_ref[p * LANES:(p + 1) * LANES, cl], jnp.zeros((LANES, LANES), BF16))
                vat = jnp.concatenate([v_ref[p, :, cl], ones_rows], axis=0)
                irow = gt_ref[3 * st:3 * st + 1, cl]
                brow = gt_ref[3 * st + 1:3 * st + 2, cl]
                bend = gt_ref[3 * st + 2:3 * st + 3, cl]
                x = _dot(g_hi, sels[p]) + _dot(g_mid, sels[p]) + _dot(g_lo, sels[p])
                dmt = jnp.where(mask, brow + x, -jnp.inf)
                cmax = jnp.max(dmt, axis=0, keepdims=True)
                p0 = (_dot(kp, qbd) * jnp.exp(dmt - cmax)).astype(BF16)
                rt0 = _dot(vat, jnp.concatenate([jnp.where(lo_half, p0, zero), jnp.where(lo_half, zero, p0)], axis=0))
                dmax = jnp.max(bend + x, axis=0, keepdims=True)
                vw = (vat.astype(F32) * jnp.exp(bend + irow - brow - dmax)).astype(BF16)
                u0 = _dot(vw, jnp.concatenate([jnp.where(lo_half, kp, zero), jnp.where(lo_half, zero, kp)], axis=0))
                m = m_ref[st][0:1, :]
                cst = c_ref[st]
                inter = brow + m
                mt = jnp.maximum(inter, cmax)
                rt = jnp.exp(cmax - mt) * rt0 + jnp.exp(inter - mt) * _dot(cst.astype(BF16), qbd)
                hts.append(rt[0:ch] / jnp.maximum(jnp.abs(rt[ch:2 * ch]), jnp.exp(-mt)))
                mnew = jnp.maximum(bend + m, dmax)
                c_ref[st] = jnp.exp(bend + m - mnew) * cst + jnp.exp(dmax - mnew) * u0
                m_ref[st] = jnp.broadcast_to(mnew, (8, LANES))
            ht = jnp.concatenate(hts, axis=0).T
            h_ref[rows, 0:LANES] = ht[0:ch]
            h_ref[rows, LANES:2 * LANES] = ht[ch:2 * ch]


def _mlstm_gate_rows(g):
    nchunk = g.shape[0] // MLSTM_CHUNK
    gc = g[:, :16].reshape(nchunk, MLSTM_CHUNK, 16)
    rows = []
    for d in range(2):
        for p in range(2):
            ci = d * 8 + 2 * p
            irow = jnp.concatenate([gc[:, :, ci], gc[:, :, ci + 1]], axis=-1)
            brow = jnp.concatenate([gc[:, :, ci + 4], gc[:, :, ci + 5]], axis=-1)
            edge = MLSTM_CHUNK - 1 if d == 0 else 0
            bend = jnp.concatenate([jnp.broadcast_to(gc[:, edge, ci + 4 + e][:, None], (nchunk, MLSTM_CHUNK))
                                    for e in range(2)], axis=-1)
            rows += [irow, brow, bend]
    rows += [jnp.zeros_like(rows[0])] * 4
    return jnp.stack(rows, axis=0).reshape(16, nchunk * LANES)


def _mlstm_call(mk, qtd, vt2, g, gt3, *, b, rs, nsc, nsl, ncb):
    n = mk.shape[0]
    nch = rs // MLSTM_CHUNK

    def fwd(bi, j):
        return jnp.where(j < nsc, bi * nsc + j, ncb + bi * nsl + (j - nsc))

    def bwd(bi, j):
        return jnp.where(j < nsc, bi * nsc + (nsc - 1 - j), ncb + bi * nsl + (nsl - 1 - (j - nsc)))

    def specs(idx):
        return [pl.BlockSpec((rs, 256), lambda bi, j: (idx(bi, j), 0)),
                pl.BlockSpec((256, nch * LANES), lambda bi, j: (0, idx(bi, j))),
                pl.BlockSpec((2, HEAD_DIM, nch * LANES), lambda bi, j: (0, 0, idx(bi, j))),
                pl.BlockSpec((rs, LANES), lambda bi, j: (idx(bi, j), 0)),
                pl.BlockSpec((16, nch * LANES), lambda bi, j: (0, idx(bi, j)))]

    return pl.pallas_call(
        functools.partial(_mlstm_kernel, nch=nch),
        grid=(b, nsc + nsl),
        in_specs=specs(fwd) + specs(bwd),
        out_specs=[pl.BlockSpec((rs, 256), lambda bi, j: (fwd(bi, j), 0)),
                   pl.BlockSpec((rs, 256), lambda bi, j: (bwd(bi, j), 0))],
        out_shape=[jax.ShapeDtypeStruct((n, 256), F32)] * 2,
        scratch_shapes=[pltpu.VMEM((MLSTM_HEADS, LANES, LANES), F32),
                        pltpu.VMEM((MLSTM_HEADS, 8, LANES), F32)],
        compiler_params=_cparams(2, 32),
        name="mlstm",
    )(mk, qtd, vt2, g, gt3, mk, qtd, vt2, g, gt3)


def _attn_kernel(q_ref, kc_ref, kl_ref, vc_ref, vl_ref, o_ref, qs_ref, m_ref, aa_ref, ab_ref, *, tq, tc, t, tk, nqc):
    lane = lax.broadcasted_iota(jnp.int32, (tq, LANES), 1)
    lo_half = lane < HEAD_DIM
    for g in range(ATTN_KV_HEADS):
        for pp in range(2):
            qp = q_ref[:, (2 * g + pp) * LANES:(2 * g + pp + 1) * LANES]
            zero = jnp.zeros_like(qp)
            qs_ref[g, pp * tq:(pp + 1) * tq, :] = jnp.where(lo_half, qp, zero)
            qs_ref[g, (2 + pp) * tq:(3 + pp) * tq, :] = jnp.where(lo_half, zero, qp)
    m_ref[...] = jnp.full_like(m_ref, NEG_BIG)
    aa_ref[...] = jnp.zeros_like(aa_ref)
    ab_ref[...] = jnp.zeros_like(ab_ref)

    def chunk(k_ref, v_ref, start, size):
        for g in range(ATTN_KV_HEADS):
            kc = k_ref[pl.ds(start, size), g * LANES:(g + 1) * LANES]
            s = _dot_nt(qs_ref[g], kc)
            m_old = m_ref[g]
            m_new = jnp.maximum(m_old, jnp.max(s, axis=-1, keepdims=True))
            alpha = jnp.exp(m_old - m_new)
            p = jnp.exp(s - jnp.concatenate([m_new] * (size // LANES), axis=1)).astype(BF16)
            va = v_ref[pl.ds(start, size), (2 * g) * LANES:(2 * g + 1) * LANES]
            vb = v_ref[pl.ds(start, size), (2 * g + 1) * LANES:(2 * g + 2) * LANES]
            aa_ref[g] = alpha[:2 * tq] * aa_ref[g] + _dot(p[:2 * tq], va)
            ab_ref[g] = alpha[2 * tq:] * ab_ref[g] + _dot(p[2 * tq:], vb)
            m_ref[g] = m_new

    for c0 in range(0, tc, tk):
        chunk(kc_ref, vc_ref, c0, min(tk, tc - c0))

    @pl.when(pl.program_id(1) >= nqc)
    def _():
        def body(i, carry):
            chunk(kl_ref, vl_ref, pl.multiple_of(i * tk, LANES), tk)
            return carry
        lax.fori_loop(0, t // tk, body, 0, unroll=2)

    for g in range(ATTN_KV_HEADS):
        for pp in range(2):
            a = aa_ref[g, pp * tq:(pp + 1) * tq, :]
            bq = ab_ref[g, pp * tq:(pp + 1) * tq, :]
            oa = a * pltpu.roll(1.0 / a, HEAD_DIM, 1)
            ob = bq * pltpu.roll(1.0 / bq, HEAD_DIM, 1)
            o_ref[:, (2 * g + pp) * LANES:(2 * g + pp + 1) * LANES] = jnp.where(lo_half, oa, ob)


def _attn_call(aq, akd, avd, *, b, tq, tc, t, tk, ncq):
    n = aq.shape[0]
    nqc, nql = tc // tq, t // tq
    nc = b * tc
    assert nc % t == 0

    def qidx(bi, j):
        return (jnp.where(j < nqc, bi * nqc + j, ncq + bi * nql + (j - nqc)), 0)

    return pl.pallas_call(
        functools.partial(_attn_kernel, tq=tq, tc=tc, t=t, tk=tk, nqc=nqc),
        grid=(b, nqc + nql),
        in_specs=[pl.BlockSpec((tq, 512), qidx),
                  pl.BlockSpec((tc, 256), lambda bi, j: (bi, 0)),
                  pl.BlockSpec((t, 256), lambda bi, j: (nc // t + bi, 0)),
                  pl.BlockSpec((tc, 512), lambda bi, j: (bi, 0)),
                  pl.BlockSpec((t, 512), lambda bi, j: (nc // t + bi, 0))],
        out_specs=pl.BlockSpec((tq, 512), qidx),
        out_shape=jax.ShapeDtypeStruct((n, 512), F32),
        scratch_shapes=[pltpu.VMEM((ATTN_KV_HEADS, 4 * tq, LANES), BF16),
                        pltpu.VMEM((ATTN_KV_HEADS, 4 * tq, LANES), F32),
                        pltpu.VMEM((ATTN_KV_HEADS, 2 * tq, LANES), F32),
                        pltpu.VMEM((ATTN_KV_HEADS, 2 * tq, LANES), F32)],
        compiler_params=_cparams(2, 48),
        name="attn",
    )(aq, akd, akd, avd, avd)


def _merge_kernel(hf_ref, hb_ref, mo_ref, gm_ref, a_ref, x_ref, mod_ref, mn_ref, wo_ref, rwh_ref, rwl_ref, rb_ref,
                  xo_ref, h2_ref, lg_ref):
    ind = _seg_indicator(2 * LANES)
    parts = [_sigmoid(mo_ref[...]) * (hf_ref[...] + hb_ref[...]), gm_ref[...],
             a_ref[:, 0:256], a_ref[:, 256:512]]
    y = jnp.concatenate([_head_rms(pt, ind) for pt in parts], axis=1) * mn_ref[...]
    xn = x_ref[...] + mod_ref[0, 2:3, :] * _dot(y.astype(BF16), wo_ref[...])
    xo_ref[...] = xn
    h2 = _rms(xn) * (1.0 + mod_ref[0, 4:5, :]) + mod_ref[0, 3:4, :]
    nsl = h2.shape[1] // LANES
    for sl in range(nsl):
        h2_ref[pl.ds(sl, h2.shape[0], stride=nsl), :] = h2[:, sl * LANES:(sl + 1) * LANES]
    hi, lo = _split(h2)
    lg_ref[...] = _dot(hi, rwh_ref[...]) + _dot(lo, rwh_ref[...]) + _dot(hi, rwl_ref[...]) + rb_ref[...]


def _merge_call(hf, hb, mo, gm, a, x, mods, mn, wo, rwh, rwl, rb, *, l, tm, nct, tpb):
    n, d = x.shape

    def mod_idx(i):
        return (l, jnp.where(i < nct, 0, 1 + (i - nct) // tpb), 0, 0)

    row = lambda i: (i, 0)
    lay = lambda i: (l, 0, 0)
    return pl.pallas_call(
        _merge_kernel,
        grid=(n // tm,),
        in_specs=[pl.BlockSpec((tm, 256), row), pl.BlockSpec((tm, 256), row), pl.BlockSpec((tm, 256), row),
                  pl.BlockSpec((tm, 256), row), pl.BlockSpec((tm, 512), row), pl.BlockSpec((tm, d), row),
                  pl.BlockSpec((None, 1, 6, d), mod_idx),
                  pl.BlockSpec((None, 1, d), lay),
                  pl.BlockSpec((None, d, d), lay),
                  pl.BlockSpec((None, d, LANES), lay),
                  pl.BlockSpec((None, d, LANES), lay),
                  pl.BlockSpec((None, 1, LANES), lay)],
        out_specs=[pl.BlockSpec((tm, d), row), pl.BlockSpec((tm * (d // LANES), LANES), row),
                   pl.BlockSpec((tm, LANES), row)],
        out_shape=[jax.ShapeDtypeStruct((n, d), F32), jax.ShapeDtypeStruct((n * (d // LANES), LANES), F32),
                   jax.ShapeDtypeStruct((n, LANES), F32)],
        compiler_params=_cparams(1, 48),
        name="merge",
    )(hf, hb, mo, gm, a, x, mods, mn, wo, rwh, rwl, rb)


def _wprep_kernel(w_ref, o_ref, *, f):
    slab = 2 * LANES
    r = lax.broadcasted_iota(jnp.int32, (slab, slab), 0)
    c = lax.broadcasted_iota(jnp.int32, (slab, slab), 1)
    perm = jnp.where(r == jnp.where(c < LANES, 2 * c, 2 * (c - LANES) + 1), 1.0, 0.0).astype(BF16)
    for s in range(2 * f // slab):
        y = _dot(w_ref[:, s * slab:(s + 1) * slab].astype(BF16), perm).astype(BF16)
        o_ref[:, s * LANES:(s + 1) * LANES] = y[:, :LANES]
        o_ref[:, f + s * LANES:f + (s + 1) * LANES] = y[:, LANES:]


def _wprep_call(w_gate_up):
    depth, ne, d, f2 = w_gate_up.shape
    rows = 512
    w2 = w_gate_up.reshape(depth * ne * d, f2)
    out = pl.pallas_call(
        functools.partial(_wprep_kernel, f=f2 // 2),
        grid=(depth * ne * d // rows,),
        in_specs=[pl.BlockSpec((rows, f2), lambda i: (i, 0))],
        out_specs=pl.BlockSpec((rows, f2), lambda i: (i, 0)),
        out_shape=jax.ShapeDtypeStruct((depth * ne * d, f2), BF16),
        compiler_params=_cparams(1, 32),
        name="wprep",
    )(w2)
    return out.reshape(depth, ne, d, f2)


def _moe_kernel(be_ref, nv_ref, tok0_ref, tokn_ref, h_hbm, wgl_ref, wd_ref, bgl_ref, bd_ref, y_ref,
                xbuf, act_ref, sem, *, tmb, f):
    i = pl.program_id(0)
    nv = nv_ref[0]
    slot = i % 2
    cw = 2 * LANES
    nchunk = f // cw
    per = tmb // (2 * nchunk)
    nsl = h_hbm.shape[1]

    def request(tok_ref, r, dst, priority=0):
        pltpu.make_async_copy(h_hbm.at[tok_ref[0, 0, r]], xbuf.at[dst, pl.ds(pl.multiple_of(r * nsl, nsl), nsl)],
                              sem.at[dst]).start(priority=priority)

    @pl.when(i == 0)
    def _():
        def issue(r, carry):
            request(tok0_ref, r, 0)
            return carry
        lax.fori_loop(0, tmb, issue, 0, unroll=8)

    @pl.when(i <= nv)
    def _():
        pltpu.make_async_copy(xbuf.at[slot], xbuf.at[slot], sem.at[slot]).wait()

    def compute(cur):
        xb = jnp.concatenate([xbuf[cur, pl.ds(sl, tmb, stride=nsl), :] for sl in range(nsl)], axis=1).astype(BF16)
        for j in range(nchunk):
            cols = slice(j * cw, (j + 1) * cw)
            ucols = slice(f + j * cw, f + (j + 1) * cw)
            x_glu = jnp.minimum(_dot(xb, wgl_ref[:, cols]) + bgl_ref[:, cols], SWIGLU_LIMIT)
            for r in range(2 * j * per, (2 * j + 1) * per):
                request(tokn_ref, r, 1 - cur, r % 2)
            x_lin = jnp.clip(_dot(xb, wgl_ref[:, ucols]) + bgl_ref[:, ucols], -SWIGLU_LIMIT, SWIGLU_LIMIT)
            for r in range((2 * j + 1) * per, (2 * j + 2) * per):
                request(tokn_ref, r, 1 - cur, r % 2)
            act_ref[:, cols] = (x_glu * _sigmoid(SWIGLU_ALPHA * x_glu) * (x_lin + 1.0)).astype(BF16)
        act = act_ref[...]
        for j in range(wd_ref.shape[1] // cw):
            cols = slice(j * cw, (j + 1) * cw)
            y = _dot(act, wd_ref[:, cols]) + bd_ref[:, cols]
            for sl in range(cw // LANES):
                y_ref[pl.ds(j * (cw // LANES) + sl, tmb, stride=nsl), :] = y[:, sl * LANES:(sl + 1) * LANES]

    for cur in range(2):
        pl.when(jnp.logical_and(i < nv, slot == cur))(functools.partial(compute, cur))

    @pl.when(i >= nv)
    def _():
        y_ref[...] = jnp.zeros_like(y_ref)


def _moe_call(block_expert, nvalid, slot_tok, h2, wgl, wd, bgl, bd, *, l, tmb):
    d = wd.shape[-1]
    nb = block_expert.shape[0]
    f = wd.shape[-2]
    wmap = lambda i, be, nv: (l, be[i], 0, 0)
    grid_spec = pltpu.PrefetchScalarGridSpec(
        num_scalar_prefetch=2,
        grid=(nb,),
        in_specs=[pl.BlockSpec((1, 1, tmb), lambda i, be, nv: (0, 0, 0), memory_space=pltpu.SMEM),
                  pl.BlockSpec((1, 1, tmb), lambda i, be, nv: (jnp.minimum(i + 1, nb - 1), 0, 0),
                               memory_space=pltpu.SMEM),
                  pl.BlockSpec(memory_space=pl.ANY),
                  pl.BlockSpec((None, None, d, 2 * f), wmap),
                  pl.BlockSpec((None, None, f, d), wmap),
                  pl.BlockSpec((None, None, 1, 2 * f), wmap),
                  pl.BlockSpec((None, None, 1, d), wmap)],
        out_specs=pl.BlockSpec((tmb * (d // LANES), LANES), lambda i, be, nv: (i, 0)),
        scratch_shapes=[pltpu.VMEM((2, tmb * (d // LANES), LANES), F32), pltpu.VMEM((tmb, f), BF16),
                        pltpu.SemaphoreType.DMA((2,))],
    )
    return pl.pallas_call(
        functools.partial(_moe_kernel, tmb=tmb, f=f),
        grid_spec=grid_spec,
        out_shape=jax.ShapeDtypeStruct((nb * tmb * (d // LANES), LANES), F32),
        compiler_params=_cparams(1, 48),
        name="moe",
    )(block_expert, nvalid, slot_tok, slot_tok, h2, wgl, wd, bgl, bd)


def _combine_kernel(dest0_ref, destn_ref, y_hbm, gate_ref, x_ref, mod_ref, o_ref, buf, sem, *, tt):
    i = pl.program_id(0)
    slot = i % 2
    nsl = y_hbm.shape[1]

    def request_tile(dest_ref, dst):
        def issue(r, carry):
            for k in range(TOP_K):
                pltpu.make_async_copy(y_hbm.at[dest_ref[0, 0, k * tt + r]],
                                      buf.at[dst, k, pl.ds(pl.multiple_of(r * nsl, nsl), nsl)],
                                      sem.at[dst]).start(priority=k % 2)
            return carry
        lax.fori_loop(0, tt, issue, 0, unroll=4)

    @pl.when(i == 0)
    def _():
        request_tile(dest0_ref, 0)

    @pl.when(i + 1 < pl.num_programs(0))
    def _():
        request_tile(destn_ref, 1 - slot)

    for k in range(TOP_K):
        pltpu.make_async_copy(buf.at[slot, k], buf.at[slot, k], sem.at[slot]).wait()
    gate = gate_ref[...]
    f = None
    for k in range(TOP_K):
        rows = jnp.concatenate([buf[slot, k, pl.ds(sl, tt, stride=nsl), :] for sl in range(nsl)], axis=1)
        g = gate[:, 2 * TOP_K + k:2 * TOP_K + k + 1]
        f = g * rows if f is None else f + g * rows
    o_ref[...] = x_ref[...] + mod_ref[0, 5:6, :] * f


def _combine_call(dest, y_slots, gates, x, mods, *, l, tt, nct, tpb):
    n, d = x.shape
    ntile = n // tt

    def mod_idx(i):
        return (l, jnp.where(i < nct, 0, 1 + (i - nct) // tpb), 0, 0)

    row = lambda i: (i, 0)
    return pl.pallas_call(
        functools.partial(_combine_kernel, tt=tt),
        grid=(ntile,),
        in_specs=[pl.BlockSpec((1, 1, tt * TOP_K), lambda i: (0, 0, 0), memory_space=pltpu.SMEM),
                  pl.BlockSpec((1, 1, tt * TOP_K), lambda i: (jnp.minimum(i + 1, ntile - 1), 0, 0),
                               memory_space=pltpu.SMEM),
                  pl.BlockSpec(memory_space=pl.ANY),
                  pl.BlockSpec((tt, LANES), row),
                  pl.BlockSpec((tt, d), row),
                  pl.BlockSpec((None, 1, 6, d), mod_idx)],
        out_specs=pl.BlockSpec((tt, d), row),
        out_shape=jax.ShapeDtypeStruct((n, d), F32),
        scratch_shapes=[pltpu.VMEM((2, TOP_K, tt * (d // LANES), LANES), F32), pltpu.SemaphoreType.DMA((2,))],
        compiler_params=_cparams(1, 32),
        name="combine",
    )(dest, dest, y_slots, gates, x, mods)


def _final_kernel(x_ref, g_ref, o_ref):
    o_ref[...] = _rms(x_ref[...]) * g_ref[...]


def _final_call(x, gain, *, tm, nct, nl):
    d = x.shape[1]
    return pl.pallas_call(
        _final_kernel,
        grid=(nl // tm,),
        in_specs=[pl.BlockSpec((tm, d), lambda i: (nct + i, 0)), pl.BlockSpec((1, d), lambda i: (0, 0))],
        out_specs=pl.BlockSpec((tm, d), lambda i: (i, 0)),
        out_shape=jax.ShapeDtypeStruct((nl, d), F32),
        compiler_params=_cparams(1, 32),
        name="final_norm",
    )(x, gain.reshape(1, d))


def _route_kernel(lg_ref, rn_ref, rt_ref, cnt_ref, run_ref, *, tr):
    @pl.when(pl.program_id(0) == 0)
    def _():
        run_ref[...] = jnp.zeros_like(run_ref)

    lane = lax.broadcasted_iota(jnp.int32, (tr, LANES), 1).astype(F32)
    lg = jnp.where(lane < N_EXPERTS, lg_ref[...], -jnp.inf)
    onehot = jnp.zeros((tr, LANES), F32)
    vals, idxs = [], []
    for _ in range(TOP_K):
        m = jnp.max(lg, axis=-1, keepdims=True)
        idx = jnp.min(jnp.where(lg == m, lane, float(LANES)), axis=-1, keepdims=True)
        hit = lane == idx
        onehot = jnp.where(hit, 1.0, onehot)
        lg = jnp.where(hit, -jnp.inf, lg)
        vals.append(m)
        idxs.append(idx)
    es = [jnp.exp(v - vals[0]) for v in vals]
    tot = es[0] + es[1] + es[2] + es[3]
    gates = [e / tot for e in es]
    r = lax.broadcasted_iota(jnp.int32, (tr, tr), 0)
    c = lax.broadcasted_iota(jnp.int32, (tr, tr), 1)
    strict = jnp.where(c < r, 1.0, 0.0).astype(BF16)
    run = run_ref[0:1, :]
    before = run + _dot(strict, onehot.astype(BF16))
    ranks = [jnp.sum(jnp.where(lane == idx, before, 0.0), axis=-1, keepdims=True) for idx in idxs]
    run_ref[...] = jnp.broadcast_to(run + jnp.sum(onehot, axis=0, keepdims=True), run_ref.shape)
    cnt_ref[...] = run_ref[...]
    out = jnp.zeros((tr, LANES), F32)
    for j, col in enumerate(idxs + ranks + gates):
        out = jnp.where(lane == float(j), col, out)
    rn_ref[...] = out
    rt_ref[...] = out.T


def _route_call(logits, *, tr):
    n = logits.shape[0]
    return pl.pallas_call(
        functools.partial(_route_kernel, tr=tr),
        grid=(n // tr,),
        in_specs=[pl.BlockSpec((tr, LANES), lambda i: (i, 0))],
        out_specs=[pl.BlockSpec((tr, LANES), lambda i: (i, 0)), pl.BlockSpec((LANES, tr), lambda i: (0, i)),
                   pl.BlockSpec((8, LANES), lambda i: (0, 0))],
        out_shape=[jax.ShapeDtypeStruct((n, LANES), F32), jax.ShapeDtypeStruct((LANES, n), F32),
                   jax.ShapeDtypeStruct((8, LANES), F32)],
        scratch_shapes=[pltpu.VMEM((8, LANES), F32)],
        compiler_params=_cparams(1, 32),
        name="route",
    )(logits)


def _slots(rt, cnt, tmb, n_blocks):
    n = rt.shape[1]
    counts = cnt[0, :N_EXPERTS].astype(jnp.int32)
    padded = (counts + tmb - 1) // tmb * tmb
    ends = jnp.cumsum(padded)
    start = ends - padded
    idx_t = rt[0:TOP_K].astype(jnp.int32)
    start_sel = jnp.zeros_like(idx_t)
    for e in range(N_EXPERTS):
        start_sel = jnp.where(idx_t == e, start[e], start_sel)
    dest_t = start_sel + rt[TOP_K:2 * TOP_K].astype(jnp.int32)
    nvalid = (ends[-1] // tmb).astype(jnp.int32)
    blk = jnp.minimum(jnp.arange(n_blocks, dtype=jnp.int32), nvalid - 1)
    block_expert = jnp.minimum(jnp.sum(((blk * tmb)[:, None] >= ends[None, :]).astype(jnp.int32), axis=1), N_EXPERTS - 1)
    nfill = n_blocks * tmb - n * TOP_K
    pad = padded - counts
    padcum = jnp.cumsum(pad)
    j = jnp.arange(nfill, dtype=jnp.int32)
    fe = jnp.sum(j[:, None] >= padcum[None, :], axis=1)
    first_free = jnp.concatenate([ends - pad, ends[-1:]])
    fill_slot = first_free[fe] + j - jnp.concatenate([padcum - pad, padcum[-1:]])[fe]
    tok = jnp.broadcast_to(jnp.arange(n, dtype=jnp.int32)[None, :], (TOP_K, n)).reshape(-1)
    _, slot_tok = lax.sort((jnp.concatenate([dest_t.reshape(-1), fill_slot]).astype(jnp.int32),
                            jnp.concatenate([tok, jnp.zeros((nfill,), jnp.int32)])), num_keys=1)
    return dest_t, slot_tok, block_expert, nvalid.reshape(1)


def _rope_tables(t, tm):
    rows = t // GRID_W
    row = jnp.repeat(jnp.arange(rows, dtype=F32), GRID_W)
    col = jnp.tile(jnp.arange(GRID_W, dtype=F32), rows)
    npair = HEAD_DIM // 4
    inv_freq = ROPE_THETA ** (-jnp.arange(npair, dtype=F32) / npair)
    ang = jnp.concatenate([row[:, None] * inv_freq, col[:, None] * inv_freq], axis=-1)
    cos = jnp.tile(jnp.cos(ang), (1, 4))
    sin = jnp.tile(jnp.concatenate([-jnp.sin(ang), jnp.sin(ang)], axis=-1), (1, 2))
    cos = jnp.concatenate([jnp.ones((tm, LANES), F32), cos], axis=0)
    sin = jnp.concatenate([jnp.zeros((tm, LANES), F32), sin], axis=0)
    return cos, sin


def _tile(nc, t, cap):
    for cand in (1024, 512, 256, 128):
        if cand <= cap and nc % cand == 0 and t % cand == 0:
            return cand
    raise ValueError("unsupported sequence lengths")


def kernel(x, c, ctx, c_ctx, w_ada, b_ada, w_in, b_in, q_norm, k_norm, gmlp_norm, gmlp_ws, gmlp_b, mix_norm, w_out,
           router_w, router_b, w_gate_up, b_gate_up, w_down, b_down, final_norm):
    b, t, d = x.shape
    tc = ctx.shape[1]
    depth = w_in.shape[0]
    nc, nl = b * tc, b * t
    n = nc + nl
    tm = _tile(nc, t, 512)
    rs = _tile(tc, t, 256)
    tq = _tile(tc, t, 256)
    tk = _tile(t, t, 512)
    tmb = 512
    tt = _tile(nc, t, 256)
    nct, tpb = nc // tm, t // tm
    n_blocks = -(-n * TOP_K // tmb) + N_EXPERTS

    def heads_0213(a, axis):
        blk = [lax.slice_in_dim(a, i * HEAD_DIM, (i + 1) * HEAD_DIM, axis=axis) for i in (0, 2, 1, 3)]
        rest = lax.slice_in_dim(a, MLSTM_HEADS * HEAD_DIM, a.shape[axis], axis=axis)
        return jnp.concatenate(blk + [rest], axis=axis)

    def cols(a):
        pad = jnp.zeros(a.shape[:-1] + (LANES - 16,), a.dtype)
        return jnp.concatenate([a[..., 0:768], heads_0213(a[..., 768:1024], a.ndim - 1), a[..., 1040:2320],
                                a[..., 1024:1040], pad], axis=-1)

    w_in_r = cols(w_in).astype(BF16)
    b_in_r = cols(b_in).reshape(depth, 1, C_END)
    qn = jnp.tile(q_norm, (1, ATTN_HEADS)).reshape(depth, 1, 512)
    kn = jnp.tile(k_norm, (1, 2)).reshape(depth, 1, LANES)
    gn = gmlp_norm.reshape(depth, 1, 256)
    ws4 = gmlp_ws.reshape(depth, GMLP_GROUPS * GMLP_CHUNK, GMLP_CHUNK).astype(BF16)
    gb = jnp.repeat(jnp.swapaxes(gmlp_b, 1, 2), HEAD_DIM, axis=2)
    mn = heads_0213(mix_norm, 1).reshape(depth, 1, d)
    wo = heads_0213(w_out, 1).astype(BF16)
    rw = jnp.pad(router_w, ((0, 0), (0, 0), (0, LANES - N_EXPERTS)))
    rwh = rw.astype(BF16)
    rwl = (rw - rwh.astype(F32)).astype(BF16)
    rb = jnp.pad(router_b, ((0, 0), (0, LANES - N_EXPERTS))).reshape(depth, 1, LANES)
    wgl = _wprep_call(w_gate_up)
    wd = w_down.astype(BF16)
    ne, ff = w_down.shape[1], w_down.shape[2]
    bgl = jnp.concatenate([b_gate_up[..., 0::2], b_gate_up[..., 1::2]], axis=-1).reshape(depth, ne, 1, 2 * ff)
    bd = b_down.reshape(depth, ne, 1, d)
    cos_t, sin_t = _rope_tables(t, tm)

    r = -(-(b + 1) // 8) * 8
    cs = jnp.zeros((r, d), F32).at[0].set(c_ctx).at[1:b + 1].set(c)
    mods = _ada_call(cs, w_ada, b_ada).reshape(depth, r, 6, d)

    xs = jnp.concatenate([ctx.reshape(nc, d), x.reshape(nl, d)], axis=0)
    for l in range(depth):
        qtd, mk, vt2, mo, g, gm, aq, akd, avd = _proj_call(
            xs, mods, w_in_r, b_in_r, qn, kn, gn, ws4, gb, cos_t, sin_t, l=l, tm=tm, nct=nct, tpb=tpb)
        gt3 = _mlstm_gate_rows(g)
        hf, hb = _mlstm_call(mk, qtd, vt2, g, gt3, b=b, rs=rs, nsc=tc // rs, nsl=t // rs, ncb=nc // rs)
        a = _attn_call(aq, akd, avd, b=b, tq=tq, tc=tc, t=t, tk=tk, ncq=nc // tq)
        xs, h2, logits = _merge_call(hf, hb, mo, gm, a, xs, mods, mn, wo, rwh, rwl, rb, l=l, tm=tm, nct=nct, tpb=tpb)
        rn, rt, cnt = _route_call(logits, tr=tm)
        dest_t, slot_tok, block_expert, nvalid = _slots(rt, cnt, tmb, n_blocks)
        y_slots = _moe_call(block_expert, nvalid, slot_tok.reshape(n_blocks, 1, tmb),
                            h2.reshape(n, d // LANES, LANES), wgl, wd, bgl, bd, l=l, tmb=tmb)
        dest_blk = dest_t.reshape(TOP_K, n // tt, tt).transpose(1, 0, 2).reshape(n // tt, 1, TOP_K * tt)
        xs = _combine_call(dest_blk, y_slots.reshape(n_blocks * tmb, d // LANES, LANES), rn, xs, mods, l=l, tt=tt,
                           nct=nc // tt, tpb=t // tt)
    out = _final_call(xs, final_norm, tm=tm, nct=nct, nl=nl)
    return out.reshape(b, t, d)
```

```python
import functools

import jax
import jax.numpy as jnp
from jax import lax
from jax.experimental import pallas as pl
from jax.experimental.pallas import tpu as pltpu

F32 = jnp.float32
BF16 = jnp.bfloat16

HEAD_DIM = 64
EPS = 1e-6
GRID_W = 64
ROPE_THETA = 10000.0
MLSTM_HEADS = 4
MLSTM_CHUNK = 64
GMLP_GROUPS = 4
GMLP_CHUNK = 128
ATTN_HEADS = 8
ATTN_KV_HEADS = 2
N_EXPERTS = 32
TOP_K = 4
SWIGLU_LIMIT = 7.0
SWIGLU_ALPHA = 1.702
LANES = 128
NEG_BIG = -1e30

C_MQ, C_MK, C_MV, C_MO, C_GU, C_GV, C_AQ, C_AK, C_AV, C_G, C_END = (
    0, 256, 512, 768, 1024, 1280, 1536, 2048, 2176, 2304, 2432)


def _dot(a, b):
    return jnp.dot(a, b, preferred_element_type=F32)


def _dot_nt(a, b):
    return lax.dot_general(a, b, (((1,), (1,)), ((), ())), preferred_element_type=F32)


def _split(x):
    hi = x.astype(BF16)
    lo = (x - hi.astype(F32)).astype(BF16)
    return hi, lo


def _cparams(ndims, vmem_mb):
    return pltpu.CompilerParams(dimension_semantics=("arbitrary",) * ndims,
                                vmem_limit_bytes=vmem_mb << 20)


def _sigmoid(x):
    return 1.0 / (1.0 + jnp.exp(-x))


def _gelu_tanh(x):
    return 0.5 * x * (1.0 + jnp.tanh(0.7978845608028654 * (x + 0.044715 * (x * x * x))))


def _rms(x):
    return x * lax.rsqrt(jnp.mean(x * x, axis=-1, keepdims=True) + EPS)


def _seg_indicator(n):
    r = lax.broadcasted_iota(jnp.int32, (n, n), 0) // HEAD_DIM
    c = lax.broadcasted_iota(jnp.int32, (n, n), 1) // HEAD_DIM
    return jnp.where(r == c, 1.0, 0.0).astype(BF16)


def _head_rms(x, ind):
    hi, lo = _split(x * x)
    ss = _dot(hi, ind) + _dot(lo, ind)
    return x * lax.rsqrt(ss * (1.0 / HEAD_DIM) + EPS)


def _ada_kernel(c_ref, w_ref, b_ref, o_ref):
    c = c_ref[...]
    s = c * _sigmoid(c)
    s_hi, s_lo = _split(s)
    w_hi, w_lo = _split(w_ref[0])
    o_ref[0] = _dot(s_hi, w_hi) + _dot(s_lo, w_hi) + _dot(s_hi, w_lo) + b_ref[0]


def _ada_call(cs, w_ada, b_ada):
    depth, d, d6 = w_ada.shape
    r = cs.shape[0]
    tn = 1536
    return pl.pallas_call(
        _ada_kernel,
        grid=(depth, d6 // tn),
        in_specs=[pl.BlockSpec((r, d), lambda l, j: (0, 0)),
                  pl.BlockSpec((1, d, tn), lambda l, j: (l, 0, j)),
                  pl.BlockSpec((1, 1, tn), lambda l, j: (l, 0, j))],
        out_specs=pl.BlockSpec((1, r, tn), lambda l, j: (l, 0, j)),
        out_shape=jax.ShapeDtypeStruct((depth, r, d6), F32),
        compiler_params=_cparams(2, 40),
        name="ada",
    )(cs, w_ada, b_ada.reshape(depth, 1, d6))


def _proj_kernel(x_ref, mod_ref, w_ref, b_ref, qn_ref, kn_ref, gn_ref, ws_ref, gb_ref, cos_ref, sin_ref,
                 qt_ref, mk_ref, vt_ref, gt_ref, mo_ref, g_ref, gm_ref, aq_ref, ak_ref, av_ref, *, tm):
    h = (_rms(x_ref[...]) * (1.0 + mod_ref[0, 1:2, :]) + mod_ref[0, 0:1, :]).astype(BF16)

    def seg(a, b):
        return _dot(h, w_ref[:, a:b]) + b_ref[:, a:b]

    lo_q = lax.broadcasted_iota(jnp.int32, (2 * LANES, LANES), 1) < MLSTM_CHUNK
    lo_v = lax.broadcasted_iota(jnp.int32, (HEAD_DIM, LANES), 1) < MLSTM_CHUNK
    qt = seg(C_MQ, C_MK).T
    vt = seg(C_MV, C_MO).T
    for m in range(tm // LANES):
        even, odd = slice(2 * m * LANES, (2 * m + 1) * LANES), slice((2 * m + 1) * LANES, (2 * m + 2) * LANES)
        qv = qt[:, m * LANES:(m + 1) * LANES]
        qr = pltpu.roll(qv, MLSTM_CHUNK, 1)
        qt_ref[:, even] = jnp.where(lo_q, qv, qr).astype(BF16)
        qt_ref[:, odd] = jnp.where(lo_q, qr, qv).astype(BF16)
        for p in range(MLSTM_HEADS // 2):
            va = vt[p * LANES:p * LANES + HEAD_DIM, m * LANES:(m + 1) * LANES]
            vb = vt[p * LANES + HEAD_DIM:(p + 1) * LANES, m * LANES:(m + 1) * LANES]
            vt_ref[p, :, even] = jnp.where(lo_v, va, pltpu.roll(vb, MLSTM_CHUNK, 1)).astype(BF16)
            vt_ref[p, :, odd] = jnp.where(lo_v, pltpu.roll(va, MLSTM_CHUNK, 1), vb).astype(BF16)
    mk_ref[...] = (seg(C_MK, C_MV) * HEAD_DIM ** -0.5).astype(BF16)
    mo_ref[...] = seg(C_MO, C_GU)

    g = seg(C_G, C_END)
    ls = jnp.minimum(g, 0.0) - jnp.log1p(jnp.exp(-jnp.abs(g)))
    r = lax.broadcasted_iota(jnp.int32, (tm, tm), 0)
    c = lax.broadcasted_iota(jnp.int32, (tm, tm), 1)
    same = (r // MLSTM_CHUNK) == (c // MLSTM_CHUNK)
    pre = jnp.where(same, jnp.where(c <= r, 1.0, 0.0), 0.0).astype(BF16)
    suf = jnp.where(same, jnp.where(c >= r, 1.0, 0.0), 0.0).astype(BF16)
    ls_hi, ls_lo = _split(ls)
    psum = _dot(pre, ls_hi) + _dot(pre, ls_lo)
    ssum = _dot(suf, ls_hi) + _dot(suf, ls_lo)
    lane = lax.broadcasted_iota(jnp.int32, (tm, LANES), 1)
    q4 = lane // MLSTM_HEADS
    gates = jnp.where(q4 == 1, psum, jnp.where(q4 == 3, ssum, g))
    g_ref[...] = gates
    gt = gates.T
    cols_a = [dd * 2 * MLSTM_HEADS + qq * MLSTM_HEADS + 2 * pp for dd in range(2) for pp in range(2) for qq in range(2)]
    ra = jnp.concatenate([gt[ca:ca + 1, :] for ca in cols_a], axis=0)
    rb = jnp.concatenate([gt[ca + 1:ca + 2, :] for ca in cols_a], axis=0)
    lo_g = lax.broadcasted_iota(jnp.int32, (8, LANES), 1) < MLSTM_CHUNK
    for m in range(tm // LANES):
        am, bm = ra[:, m * LANES:(m + 1) * LANES], rb[:, m * LANES:(m + 1) * LANES]
        gt_ref[:, 2 * m * LANES:(2 * m + 1) * LANES] = jnp.where(lo_g, am, pltpu.roll(bm, MLSTM_CHUNK, 1))
        gt_ref[:, (2 * m + 1) * LANES:(2 * m + 2) * LANES] = jnp.where(lo_g, pltpu.roll(am, MLSTM_CHUNK, 1), bm)

    u = _gelu_tanh(seg(C_GU, C_GV))
    v = (_rms(_gelu_tanh(seg(C_GV, C_AQ))) * gn_ref[...]).astype(BF16)
    grp = lax.broadcasted_iota(jnp.int32, (GMLP_CHUNK, 2 * LANES), 1) // HEAD_DIM
    for ci in range(tm // GMLP_CHUNK):
        rows = slice(ci * GMLP_CHUNK, (ci + 1) * GMLP_CHUNK)
        full = _dot(ws_ref[...], v[rows, :])
        mixed = gb_ref[...]
        for gi in range(GMLP_GROUPS):
            mixed = mixed + jnp.where(grp == gi, full[gi * GMLP_CHUNK:(gi + 1) * GMLP_CHUNK, :], 0.0)
        gm_ref[rows, :] = u[rows, :] * mixed

    cos = cos_ref[...]
    sin = sin_ref[...]
    first_half = (lane % HEAD_DIM) < (HEAD_DIM // 2)
    lo_half = lane < HEAD_DIM

    def rope(blk):
        partner = jnp.where(first_half, pltpu.roll(blk, LANES - HEAD_DIM // 2, 1),
                            pltpu.roll(blk, HEAD_DIM // 2, 1))
        return blk * cos + partner * sin

    ind256 = _seg_indicator(2 * LANES)
    for s in range(2):
        qn = _head_rms(seg(C_AQ + s * 256, C_AQ + (s + 1) * 256), ind256) * qn_ref[:, s * 256:(s + 1) * 256]
        for j in range(2):
            blk = rope(qn[:, j * LANES:(j + 1) * LANES]) * HEAD_DIM ** -0.5
            aq_ref[:, s * 256 + j * LANES: s * 256 + (j + 1) * LANES] = blk.astype(BF16)

    kr = rope(_head_rms(seg(C_AK, C_AV), ind256[:LANES, :LANES]) * kn_ref[...])
    kroll = pltpu.roll(kr, HEAD_DIM, 1)
    ak_ref[:, 0:LANES] = jnp.where(lo_half, kr, kroll).astype(BF16)
    ak_ref[:, LANES:2 * LANES] = jnp.where(lo_half, kroll, kr).astype(BF16)

    vf = seg(C_AV, C_G)
    vroll = pltpu.roll(vf, HEAD_DIM, 1)
    av_ref[:, 0:LANES] = jnp.where(lo_half, vf, 1.0).astype(BF16)
    av_ref[:, LANES:2 * LANES] = jnp.where(lo_half, 1.0, vroll).astype(BF16)
    av_ref[:, 2 * LANES:3 * LANES] = jnp.where(lo_half, vroll, 1.0).astype(BF16)
    av_ref[:, 3 * LANES:4 * LANES] = jnp.where(lo_half, 1.0, vf).astype(BF16)


def _proj_call(x, mods, w, b, qn, kn, gn, ws4, gb, cos_t, sin_t, *, l, tm, nct, tpb):
    n, d = x.shape

    def mod_idx(i):
        return (l, jnp.where(i < nct, 0, 1 + (i - nct) // tpb), 0, 0)

    def rope_idx(i):
        return (jnp.where(i < nct, 0, 1 + (i - nct) % tpb), 0)

    row = lambda i: (i, 0)
    const2 = lambda i: (0, 0)
    lay = lambda i: (l, 0, 0)
    nlane = tm // MLSTM_CHUNK * LANES
    outs = [(256, BF16), (256, F32), (LANES, F32), (256, F32), (512, BF16), (256, BF16), (512, BF16)]
    row_specs = [pl.BlockSpec((tm, wd), row) for wd, _ in outs]
    row_shapes = [jax.ShapeDtypeStruct((n, wd), dt) for wd, dt in outs]
    qt_spec = pl.BlockSpec((256, nlane), lambda i: (0, i))
    vt_spec = pl.BlockSpec((MLSTM_HEADS // 2, HEAD_DIM, nlane), lambda i: (0, 0, i))
    gt_spec = pl.BlockSpec((8, nlane), lambda i: (0, i))
    gt_shape = jax.ShapeDtypeStruct((8, n // MLSTM_CHUNK * LANES), F32)
    qt_shape = jax.ShapeDtypeStruct((256, n // MLSTM_CHUNK * LANES), BF16)
    vt_shape = jax.ShapeDtypeStruct((MLSTM_HEADS // 2, HEAD_DIM, n // MLSTM_CHUNK * LANES), BF16)
    return pl.pallas_call(
        functools.partial(_proj_kernel, tm=tm),
        grid=(n // tm,),
        in_specs=[pl.BlockSpec((tm, d), row),
                  pl.BlockSpec((None, 1, 6, d), mod_idx),
                  pl.BlockSpec((None, d, C_END), lay),
                  pl.BlockSpec((None, 1, C_END), lay),
                  pl.BlockSpec((None, 1, 512), lay),
                  pl.BlockSpec((None, 1, LANES), lay),
                  pl.BlockSpec((None, 1, 256), lay),
                  pl.BlockSpec((None, GMLP_GROUPS * GMLP_CHUNK, GMLP_CHUNK), lay),
                  pl.BlockSpec((None, GMLP_CHUNK, 256), lay),
                  pl.BlockSpec((tm, LANES), rope_idx),
                  pl.BlockSpec((tm, LANES), rope_idx)],
        out_specs=[qt_spec, row_specs[0], vt_spec, gt_spec] + row_specs[1:],
        out_shape=[qt_shape, row_shapes[0], vt_shape, gt_shape] + row_shapes[1:],
        compiler_params=_cparams(1, 48),
        name="proj",
    )(x, mods, w, b, qn, kn, gn, ws4, gb, cos_t, sin_t)


def _mlstm_kernel(kf_ref, qf_ref, vf_ref, gf_ref, gtf_ref, kb_ref, qb_ref, vb_ref, gb_ref, gtb_ref,
                  hf_ref, hb_ref, c_ref, m_ref, *, nch):
    ch = MLSTM_CHUNK

    @pl.when(pl.program_id(1) == 0)
    def _():
        c_ref[...] = jnp.zeros_like(c_ref)
        m_ref[...] = jnp.zeros_like(m_ref)

    sub = lax.broadcasted_iota(jnp.int32, (ch, LANES), 0)
    lane = lax.broadcasted_iota(jnp.int32, (ch, LANES), 1)
    lo_half = lane < ch
    pos = lane % ch
    r2 = lax.broadcasted_iota(jnp.int32, (LANES, LANES), 0)
    c2 = lax.broadcasted_iota(jnp.int32, (LANES, LANES), 1)
    blockdiag = (r2 < ch) == (c2 < ch)
    ones_rows = jnp.ones((ch, LANES), BF16)
    dirs = ((kf_ref, qf_ref, vf_ref, gf_ref, gtf_ref, hf_ref), (kb_ref, qb_ref, vb_ref, gb_ref, gtb_ref, hb_ref))
    for d, (k_ref, q_ref, v_ref, g_ref, gt_ref, h_ref) in enumerate(dirs):
        mask = (sub <= pos) if d == 0 else (sub >= pos)
        sels = []
        for p in range(MLSTM_HEADS // 2):
            gi = d * 2 * MLSTM_HEADS + 2 * p + jnp.where(c2 >= ch, 1, 0)
            sels.append((jnp.where(r2 == gi, 1.0, 0.0) - jnp.where(r2 == gi + MLSTM_HEADS, 1.0, 0.0)).astype(BF16))
        order = range(nch) if d == 0 else range(nch - 1, -1, -1)
        for ci in order:
            rows = slice(ci * ch, (ci + 1) * ch)
            cl = slice(ci * LANES, (ci + 1) * LANES)
            g = g_ref[rows, :]
            g_hi = g.astype(BF16)
            g_r = g - g_hi.astype(F32)
            g_mid = g_r.astype(BF16)
            g_lo = (g_r - g_mid.astype(F32)).astype(BF16)
            hts = []
            for p in range(MLSTM_HEADS // 2):
                st = d * (MLSTM_HEADS // 2) + p
                kp = k_ref[rows, p * LANES:(p + 1) * LANES]
                zero = jnp.zeros_like(kp)
                qbd = jnp.where(blockdiag, q_ref[p * LANES:(p + 1) * LANES, cl], jnp.zeros((LANES, LANES), BF16))
                vat = jnp.concatenate([v_ref[p, :, cl], ones_rows], axis=0)
                irow = gt_ref[2 * st:2 * st + 1, cl]
                brow = gt_ref[2 * st + 1:2 * st + 2, cl]
                ea, eb = (ch - 1, 2 * ch - 1) if d == 0 else (0, ch)
                bend = jnp.where(lo_half[0:1], brow[:, ea:ea + 1], brow[:, eb:eb + 1])
                x = _dot(g_hi, sels[p]) + _dot(g_mid, sels[p]) + _dot(g_lo, sels[p])
                dmt = jnp.where(mask, brow + x, -jnp.inf)
                cmax = jnp.max(dmt, axis=0, keepdims=True)
                p0 = (_dot(kp, qbd) * jnp.exp(dmt - cmax)).astype(BF16)
                rt0 = _dot(vat, jnp.concatenate([jnp.where(lo_half, p0, zero), jnp.where(lo_half, zero, p0)], axis=0))
                dmax = jnp.max(bend + x, axis=0, keepdims=True)
                vw = (vat.astype(F32) * jnp.exp(bend + irow - brow - dmax)).astype(BF16)
                u0 = _dot(vw, jnp.concatenate([jnp.where(lo_half, kp, zero), jnp.where(lo_half, zero, kp)], axis=0))
                m = m_ref[st][0:1, :]
                cst = c_ref[st]
                inter = brow + m
                mt = jnp.maximum(inter, cmax)
                rt = jnp.exp(cmax - mt) * rt0 + jnp.exp(inter - mt) * _dot(cst.astype(BF16), qbd)
                hts.append(rt[0:ch] / jnp.maximum(jnp.abs(rt[ch:2 * ch]), jnp.exp(-mt)))
                mnew = jnp.maximum(bend + m, dmax)
                c_ref[st] = jnp.exp(bend + m - mnew) * cst + jnp.exp(dmax - mnew) * u0
                m_ref[st] = jnp.broadcast_to(mnew, (8, LANES))
            ht = jnp.concatenate(hts, axis=0).T
            h_ref[rows, 0:LANES] = ht[0:ch]
            h_ref[rows, LANES:2 * LANES] = ht[ch:2 * ch]


def _mlstm_call(mk, qtd, vt2, g, gt3, *, b, rs, nsc, nsl, ncb):
    n = mk.shape[0]
    nch = rs // MLSTM_CHUNK

    def fwd(bi, j):
        return jnp.where(j < nsc, bi * nsc + j, ncb + bi * nsl + (j - nsc))

    def bwd(bi, j):
        return jnp.where(j < nsc, bi * nsc + (nsc - 1 - j), ncb + bi * nsl + (nsl - 1 - (j - nsc)))

    def specs(idx):
        return [pl.BlockSpec((rs, 256), lambda bi, j: (idx(bi, j), 0)),
                pl.BlockSpec((256, nch * LANES), lambda bi, j: (0, idx(bi, j))),
                pl.BlockSpec((2, HEAD_DIM, nch * LANES), lambda bi, j: (0, 0, idx(bi, j))),
                pl.BlockSpec((rs, LANES), lambda bi, j: (idx(bi, j), 0)),
                pl.BlockSpec((8, nch * LANES), lambda bi, j: (0, idx(bi, j)))]

    return pl.pallas_call(
        functools.partial(_mlstm_kernel, nch=nch),
        grid=(b, nsc + nsl),
        in_specs=specs(fwd) + specs(bwd),
        out_specs=[pl.BlockSpec((rs, 256), lambda bi, j: (fwd(bi, j), 0)),
                   pl.BlockSpec((rs, 256), lambda bi, j: (bwd(bi, j), 0))],
        out_shape=[jax.ShapeDtypeStruct((n, 256), F32)] * 2,
        scratch_shapes=[pltpu.VMEM((MLSTM_HEADS, LANES, LANES), F32),
                        pltpu.VMEM((MLSTM_HEADS, 8, LANES), F32)],
        compiler_params=_cparams(2, 32),
        name="mlstm",
    )(mk, qtd, vt2, g, gt3, mk, qtd, vt2, g, gt3)


def _attn_kernel(q_ref, kc_ref, kl_ref, vc_ref, vl_ref, o_ref, qs_ref, m_ref, aa_ref, ab_ref, *, tq, tc, t, tk, nqc):
    lane = lax.broadcasted_iota(jnp.int32, (tq, LANES), 1)
    lo_half = lane < HEAD_DIM
    for g in range(ATTN_KV_HEADS):
        for pp in range(2):
            qp = q_ref[:, (2 * g + pp) * LANES:(2 * g + pp + 1) * LANES]
            zero = jnp.zeros_like(qp)
            qs_ref[g, pp * tq:(pp + 1) * tq, :] = jnp.where(lo_half, qp, zero)
            qs_ref[g, (2 + pp) * tq:(3 + pp) * tq, :] = jnp.where(lo_half, zero, qp)
    m_ref[...] = jnp.full_like(m_ref, NEG_BIG)
    aa_ref[...] = jnp.zeros_like(aa_ref)
    ab_ref[...] = jnp.zeros_like(ab_ref)

    def chunk(k_ref, v_ref, start, size):
        for g in range(ATTN_KV_HEADS):
            kc = k_ref[pl.ds(start, size), g * LANES:(g + 1) * LANES]
            s = _dot_nt(qs_ref[g], kc)
            m_old = m_ref[g]
            m_new = jnp.maximum(m_old, jnp.max(s, axis=-1, keepdims=True))
            alpha = jnp.exp(m_old - m_new)
            p = jnp.exp(s - jnp.concatenate([m_new] * (size // LANES), axis=1)).astype(BF16)
            va = v_ref[pl.ds(start, size), (2 * g) * LANES:(2 * g + 1) * LANES]
            vb = v_ref[pl.ds(start, size), (2 * g + 1) * LANES:(2 * g + 2) * LANES]
            aa_ref[g] = alpha[:2 * tq] * aa_ref[g] + _dot(p[:2 * tq], va)
            ab_ref[g] = alpha[2 * tq:] * ab_ref[g] + _dot(p[2 * tq:], vb)
            m_ref[g] = m_new

    for c0 in range(0, tc, tk):
        chunk(kc_ref, vc_ref, c0, min(tk, tc - c0))

    @pl.when(pl.program_id(1) >= nqc)
    def _():
        def body(i, carry):
            chunk(kl_ref, vl_ref, pl.multiple_of(i * tk, LANES), tk)
            return carry
        lax.fori_loop(0, t // tk, body, 0, unroll=2)

    for g in range(ATTN_KV_HEADS):
        for pp in range(2):
            a = aa_ref[g, pp * tq:(pp + 1) * tq, :]
            bq = ab_ref[g, pp * tq:(pp + 1) * tq, :]
            oa = a * pltpu.roll(1.0 / a, HEAD_DIM, 1)
            ob = bq * pltpu.roll(1.0 / bq, HEAD_DIM, 1)
            o_ref[:, (2 * g + pp) * LANES:(2 * g + pp + 1) * LANES] = jnp.where(lo_half, oa, ob)


def _attn_call(aq, akd, avd, *, b, tq, tc, t, tk, ncq):
    n = aq.shape[0]
    nqc, nql = tc // tq, t // tq
    nc = b * tc
    assert nc % t == 0

    def qidx(bi, j):
        return (jnp.where(j < nqc, bi * nqc + j, ncq + bi * nql + (j - nqc)), 0)

    return pl.pallas_call(
        functools.partial(_attn_kernel, tq=tq, tc=tc, t=t, tk=tk, nqc=nqc),
        grid=(b, nqc + nql),
        in_specs=[pl.BlockSpec((tq, 512), qidx),
                  pl.BlockSpec((tc, 256), lambda bi, j: (bi, 0)),
                  pl.BlockSpec((t, 256), lambda bi, j: (nc // t + bi, 0)),
                  pl.BlockSpec((tc, 512), lambda bi, j: (bi, 0)),
                  pl.BlockSpec((t, 512), lambda bi, j: (nc // t + bi, 0))],
        out_specs=pl.BlockSpec((tq, 512), qidx),
        out_shape=jax.ShapeDtypeStruct((n, 512), F32),
        scratch_shapes=[pltpu.VMEM((ATTN_KV_HEADS, 4 * tq, LANES), BF16),
                        pltpu.VMEM((ATTN_KV_HEADS, 4 * tq, LANES), F32),
                        pltpu.VMEM((ATTN_KV_HEADS, 2 * tq, LANES), F32),
                        pltpu.VMEM((ATTN_KV_HEADS, 2 * tq, LANES), F32)],
        compiler_params=_cparams(2, 48),
        name="attn",
    )(aq, akd, akd, avd, avd)


def _merge_kernel(hf_ref, hb_ref, mo_ref, gm_ref, a_ref, x_ref, mod_ref, mn_ref, wo_ref, rwh_ref, rwl_ref, rb_ref,
                  xo_ref, h2_ref, lg_ref):
    ind = _seg_indicator(2 * LANES)
    parts = [_sigmoid(mo_ref[...]) * (hf_ref[...] + hb_ref[...]), gm_ref[...],
             a_ref[:, 0:256], a_ref[:, 256:512]]
    y = jnp.concatenate([_head_rms(pt, ind) for pt in parts], axis=1) * mn_ref[...]
    xn = x_ref[...] + mod_ref[0, 2:3, :] * _dot(y.astype(BF16), wo_ref[...])
    xo_ref[...] = xn
    h2 = _rms(xn) * (1.0 + mod_ref[0, 4:5, :]) + mod_ref[0, 3:4, :]
    nsl = h2.shape[1] // LANES
    for sl in range(nsl):
        h2_ref[pl.ds(sl, h2.shape[0], stride=nsl), :] = h2[:, sl * LANES:(sl + 1) * LANES]
    hi, lo = _split(h2)
    lg_ref[...] = _dot(hi, rwh_ref[...]) + _dot(lo, rwh_ref[...]) + _dot(hi, rwl_ref[...]) + rb_ref[...]


def _merge_call(hf, hb, mo, gm, a, x, mods, mn, wo, rwh, rwl, rb, *, l, tm, nct, tpb):
    n, d = x.shape

    def mod_idx(i):
        return (l, jnp.where(i < nct, 0, 1 + (i - nct) // tpb), 0, 0)

    row = lambda i: (i, 0)
    lay = lambda i: (l, 0, 0)
    return pl.pallas_call(
        _merge_kernel,
        grid=(n // tm,),
        in_specs=[pl.BlockSpec((tm, 256), row), pl.BlockSpec((tm, 256), row), pl.BlockSpec((tm, 256), row),
                  pl.BlockSpec((tm, 256), row), pl.BlockSpec((tm, 512), row), pl.BlockSpec((tm, d), row),
                  pl.BlockSpec((None, 1, 6, d), mod_idx),
                  pl.BlockSpec((None, 1, d), lay),
                  pl.BlockSpec((None, d, d), lay),
                  pl.BlockSpec((None, d, LANES), lay),
                  pl.BlockSpec((None, d, LANES), lay),
                  pl.BlockSpec((None, 1, LANES), lay)],
        out_specs=[pl.BlockSpec((tm, d), row), pl.BlockSpec((tm * (d // LANES), LANES), row),
                   pl.BlockSpec((tm, LANES), row)],
        out_shape=[jax.ShapeDtypeStruct((n, d), F32), jax.ShapeDtypeStruct((n * (d // LANES), LANES), F32),
                   jax.ShapeDtypeStruct((n, LANES), F32)],
        compiler_params=_cparams(1, 48),
        name="merge",
    )(hf, hb, mo, gm, a, x, mods, mn, wo, rwh, rwl, rb)


def _wprep_kernel(w_ref, o_ref, *, f):
    slab = 2 * LANES
    r = lax.broadcasted_iota(jnp.int32, (slab, slab), 0)
    c = lax.broadcasted_iota(jnp.int32, (slab, slab), 1)
    perm = jnp.where(r == jnp.where(c < LANES, 2 * c, 2 * (c - LANES) + 1), 1.0, 0.0).astype(BF16)
    for s in range(2 * f // slab):
        y = _dot(w_ref[:, s * slab:(s + 1) * slab].astype(BF16), perm).astype(BF16)
        o_ref[:, s * LANES:(s + 1) * LANES] = y[:, :LANES]
        o_ref[:, f + s * LANES:f + (s + 1) * LANES] = y[:, LANES:]


def _wprep_call(w_gate_up):
    depth, ne, d, f2 = w_gate_up.shape
    rows = 512
    w2 = w_gate_up.reshape(depth * ne * d, f2)
    out = pl.pallas_call(
        functools.partial(_wprep_kernel, f=f2 // 2),
        grid=(depth * ne * d // rows,),
        in_specs=[pl.BlockSpec((rows, f2), lambda i: (i, 0))],
        out_specs=pl.BlockSpec((rows, f2), lambda i: (i, 0)),
        out_shape=jax.ShapeDtypeStruct((depth * ne * d, f2), BF16),
        compiler_params=_cparams(1, 32),
        name="wprep",
    )(w2)
    return out.reshape(depth, ne, d, f2)


def _moe_kernel(be_ref, nv_ref, tok0_ref, tokn_ref, h_hbm, wgl_ref, wd_ref, bgl_ref, bd_ref, y_ref,
                xbuf, act_ref, sem, *, tmb, f):
    i = pl.program_id(0)
    nv = nv_ref[0]
    slot = i % 2
    cw = 2 * LANES
    nchunk = f // cw
    per = tmb // (2 * nchunk)
    nsl = h_hbm.shape[1]

    def request(tok_ref, r, dst, priority=0):
        pltpu.make_async_copy(h_hbm.at[tok_ref[0, 0, r]], xbuf.at[dst, pl.ds(pl.multiple_of(r * nsl, nsl), nsl)],
                              sem.at[dst]).start(priority=priority)

    @pl.when(i == 0)
    def _():
        def issue(r, carry):
            request(tok0_ref, r, 0)
            return carry
        lax.fori_loop(0, tmb, issue, 0, unroll=8)

    @pl.when(i <= nv)
    def _():
        pltpu.make_async_copy(xbuf.at[slot], xbuf.at[slot], sem.at[slot]).wait()

    def compute(cur):
        xb = jnp.concatenate([xbuf[cur, pl.ds(sl, tmb, stride=nsl), :] for sl in range(nsl)], axis=1).astype(BF16)
        for j in range(nchunk):
            cols = slice(j * cw, (j + 1) * cw)
            ucols = slice(f + j * cw, f + (j + 1) * cw)
            x_glu = jnp.minimum(_dot(xb, wgl_ref[:, cols]) + bgl_ref[:, cols], SWIGLU_LIMIT)
            for r in range(2 * j * per, (2 * j + 1) * per):
                request(tokn_ref, r, 1 - cur, r % 2)
            x_lin = jnp.clip(_dot(xb, wgl_ref[:, ucols]) + bgl_ref[:, ucols], -SWIGLU_LIMIT, SWIGLU_LIMIT)
            for r in range((2 * j + 1) * per, (2 * j + 2) * per):
                request(tokn_ref, r, 1 - cur, r % 2)
            act_ref[:, cols] = (x_glu * _sigmoid(SWIGLU_ALPHA * x_glu) * (x_lin + 1.0)).astype(BF16)
        act = act_ref[...]
        for j in range(wd_ref.shape[1] // cw):
            cols = slice(j * cw, (j + 1) * cw)
            y = _dot(act, wd_ref[:, cols]) + bd_ref[:, cols]
            for sl in range(cw // LANES):
                y_ref[pl.ds(j * (cw // LANES) + sl, tmb, stride=nsl), :] = y[:, sl * LANES:(sl + 1) * LANES]

    for cur in range(2):
        pl.when(jnp.logical_and(i < nv, slot == cur))(functools.partial(compute, cur))

    @pl.when(i >= nv)
    def _():
        y_ref[...] = jnp.zeros_like(y_ref)


def _moe_call(block_expert, nvalid, slot_tok, h2, wgl, wd, bgl, bd, *, l, tmb):
    d = wd.shape[-1]
    nb = block_expert.shape[0]
    f = wd.shape[-2]
    wmap = lambda i, be, nv: (l, be[i], 0, 0)
    grid_spec = pltpu.PrefetchScalarGridSpec(
        num_scalar_prefetch=2,
        grid=(nb,),
        in_specs=[pl.BlockSpec((1, 1, tmb), lambda i, be, nv: (0, 0, 0), memory_space=pltpu.SMEM),
                  pl.BlockSpec((1, 1, tmb), lambda i, be, nv: (jnp.minimum(i + 1, nb - 1), 0, 0),
                               memory_space=pltpu.SMEM),
                  pl.BlockSpec(memory_space=pl.ANY),
                  pl.BlockSpec((None, None, d, 2 * f), wmap),
                  pl.BlockSpec((None, None, f, d), wmap),
                  pl.BlockSpec((None, None, 1, 2 * f), wmap),
                  pl.BlockSpec((None, None, 1, d), wmap)],
        out_specs=pl.BlockSpec((tmb * (d // LANES), LANES), lambda i, be, nv: (i, 0)),
        scratch_shapes=[pltpu.VMEM((2, tmb * (d // LANES), LANES), F32), pltpu.VMEM((tmb, f), BF16),
                        pltpu.SemaphoreType.DMA((2,))],
    )
    return pl.pallas_call(
        functools.partial(_moe_kernel, tmb=tmb, f=f),
        grid_spec=grid_spec,
        out_shape=jax.ShapeDtypeStruct((nb * tmb * (d // LANES), LANES), F32),
        compiler_params=_cparams(1, 48),
        name="moe",
    )(block_expert, nvalid, slot_tok, slot_tok, h2, wgl, wd, bgl, bd)


def _combine_kernel(dest0_ref, destn_ref, y_hbm, gate_ref, x_ref, mod_ref, o_ref, buf, sem, *, tt):
    i = pl.program_id(0)
    slot = i % 2
    nsl = y_hbm.shape[1]

    def request_tile(dest_ref, dst):
        def issue(r, carry):
            for k in range(TOP_K):
                pltpu.make_async_copy(y_hbm.at[dest_ref[0, 0, k * tt + r]],
                                      buf.at[dst, k, pl.ds(pl.multiple_of(r * nsl, nsl), nsl)],
                                      sem.at[dst]).start(priority=k % 2)
            return carry
        lax.fori_loop(0, tt, issue, 0, unroll=4)

    @pl.when(i == 0)
    def _():
        request_tile(dest0_ref, 0)

    @pl.when(i + 1 < pl.num_programs(0))
    def _():
        request_tile(destn_ref, 1 - slot)

    for k in range(TOP_K):
        pltpu.make_async_copy(buf.at[slot, k], buf.at[slot, k], sem.at[slot]).wait()
    gate = gate_ref[...]
    f = None
    for k in range(TOP_K):
        rows = jnp.concatenate([buf[slot, k, pl.ds(sl, tt, stride=nsl), :] for sl in range(nsl)], axis=1)
        g = gate[:, 2 * TOP_K + k:2 * TOP_K + k + 1]
        f = g * rows if f is None else f + g * rows
    o_ref[...] = x_ref[...] + mod_ref[0, 5:6, :] * f


def _combine_call(dest, y_slots, gates, x, mods, *, l, tt, nct, tpb):
    n, d = x.shape
    ntile = n // tt

    def mod_idx(i):
        return (l, jnp.where(i < nct, 0, 1 + (i - nct) // tpb), 0, 0)

    row = lambda i: (i, 0)
    return pl.pallas_call(
        functools.partial(_combine_kernel, tt=tt),
        grid=(ntile,),
        in_specs=[pl.BlockSpec((1, 1, tt * TOP_K), lambda i: (0, 0, 0), memory_space=pltpu.SMEM),
                  pl.BlockSpec((1, 1, tt * TOP_K), lambda i: (jnp.minimum(i + 1, ntile - 1), 0, 0),
                               memory_space=pltpu.SMEM),
                  pl.BlockSpec(memory_space=pl.ANY),
                  pl.BlockSpec((tt, LANES), row),
                  pl.BlockSpec((tt, d), row),
                  pl.BlockSpec((None, 1, 6, d), mod_idx)],
        out_specs=pl.BlockSpec((tt, d), row),
        out_shape=jax.ShapeDtypeStruct((n, d), F32),
        scratch_shapes=[pltpu.VMEM((2, TOP_K, tt * (d // LANES), LANES), F32), pltpu.SemaphoreType.DMA((2,))],
        compiler_params=_cparams(1, 32),
        name="combine",
    )(dest, dest, y_slots, gates, x, mods)


def _final_kernel(x_ref, g_ref, o_ref):
    o_ref[...] = _rms(x_ref[...]) * g_ref[...]


def _final_call(x, gain, *, tm, nct, nl):
    d = x.shape[1]
    return pl.pallas_call(
        _final_kernel,
        grid=(nl // tm,),
        in_specs=[pl.BlockSpec((tm, d), lambda i: (nct + i, 0)), pl.BlockSpec((1, d), lambda i: (0, 0))],
        out_specs=pl.BlockSpec((tm, d), lambda i: (i, 0)),
        out_shape=jax.ShapeDtypeStruct((nl, d), F32),
        compiler_params=_cparams(1, 32),
        name="final_norm",
    )(x, gain.reshape(1, d))


def _route_kernel(lg_ref, rn_ref, rt_ref, cnt_ref, run_ref, *, tr):
    @pl.when(pl.program_id(0) == 0)
    def _():
        run_ref[...] = jnp.zeros_like(run_ref)

    lane = lax.broadcasted_iota(jnp.int32, (tr, LANES), 1).astype(F32)
    lg = jnp.where(lane < N_EXPERTS, lg_ref[...], -jnp.inf)
    onehot = jnp.zeros((tr, LANES), F32)
    vals, idxs = [], []
    for _ in range(TOP_K):
        m = jnp.max(lg, axis=-1, keepdims=True)
        idx = jnp.min(jnp.where(lg == m, lane, float(LANES)), axis=-1, keepdims=True)
        hit = lane == idx
        onehot = jnp.where(hit, 1.0, onehot)
        lg = jnp.where(hit, -jnp.inf, lg)
        vals.append(m)
        idxs.append(idx)
    es = [jnp.exp(v - vals[0]) for v in vals]
    tot = es[0] + es[1] + es[2] + es[3]
    gates = [e / tot for e in es]
    r = lax.broadcasted_iota(jnp.int32, (tr, tr), 0)
    c = lax.broadcasted_iota(jnp.int32, (tr, tr), 1)
    strict = jnp.where(c < r, 1.0, 0.0).astype(BF16)
    run = run_ref[0:1, :]
    before = run + _dot(strict, onehot.astype(BF16))
    ranks = [jnp.sum(jnp.where(lane == idx, before, 0.0), axis=-1, keepdims=True) for idx in idxs]
    run_ref[...] = jnp.broadcast_to(run + jnp.sum(onehot, axis=0, keepdims=True), run_ref.shape)
    cnt_ref[...] = run_ref[...]
    out = jnp.zeros((tr, LANES), F32)
    for j, col in enumerate(idxs + ranks + gates):
        out = jnp.where(lane == float(j), col, out)
    rn_ref[...] = out
    rt_ref[...] = out.T


def _route_call(logits, *, tr):
    n = logits.shape[0]
    return pl.pallas_call(
        functools.partial(_route_kernel, tr=tr),
        grid=(n // tr,),
        in_specs=[pl.BlockSpec((tr, LANES), lambda i: (i, 0))],
        out_specs=[pl.BlockSpec((tr, LANES), lambda i: (i, 0)), pl.BlockSpec((LANES, tr), lambda i: (0, i)),
                   pl.BlockSpec((8, LANES), lambda i: (0, 0))],
        out_shape=[jax.ShapeDtypeStruct((n, LANES), F32), jax.ShapeDtypeStruct((LANES, n), F32),
                   jax.ShapeDtypeStruct((8, LANES), F32)],
        scratch_shapes=[pltpu.VMEM((8, LANES), F32)],
        compiler_params=_cparams(1, 32),
        name="route",
    )(logits)


def _slots(rt, cnt, tmb, n_blocks):
    n = rt.shape[1]
    counts = cnt[0, :N_EXPERTS].astype(jnp.int32)
    padded = (counts + tmb - 1) // tmb * tmb
    ends = jnp.cumsum(padded)
    start = ends - padded
    idx_t = rt[0:TOP_K].astype(jnp.int32)
    start_sel = jnp.zeros_like(idx_t)
    for e in range(N_EXPERTS):
        start_sel = jnp.where(idx_t == e, start[e], start_sel)
    dest_t = start_sel + rt[TOP_K:2 * TOP_K].astype(jnp.int32)
    nvalid = (ends[-1] // tmb).astype(jnp.int32)
    blk = jnp.minimum(jnp.arange(n_blocks, dtype=jnp.int32), nvalid - 1)
    block_expert = jnp.minimum(jnp.sum(((blk * tmb)[:, None] >= ends[None, :]).astype(jnp.int32), axis=1), N_EXPERTS - 1)
    nfill = n_blocks * tmb - n * TOP_K
    pad = padded - counts
    padcum = jnp.cumsum(pad)
    j = jnp.arange(nfill, dtype=jnp.int32)
    fe = jnp.sum(j[:, None] >= padcum[None, :], axis=1)
    first_free = jnp.concatenate([ends - pad, ends[-1:]])
    fill_slot = first_free[fe] + j - jnp.concatenate([padcum - pad, padcum[-1:]])[fe]
    tok = jnp.broadcast_to(jnp.arange(n, dtype=jnp.int32)[None, :], (TOP_K, n)).reshape(-1)
    _, slot_tok = lax.sort((jnp.concatenate([dest_t.reshape(-1), fill_slot]).astype(jnp.int32),
                            jnp.concatenate([tok, jnp.zeros((nfill,), jnp.int32)])), num_keys=1)
    return dest_t, slot_tok, block_expert, nvalid.reshape(1)


def _rope_tables(t, tm):
    rows = t // GRID_W
    row = jnp.repeat(jnp.arange(rows, dtype=F32), GRID_W)
    col = jnp.tile(jnp.arange(GRID_W, dtype=F32), rows)
    npair = HEAD_DIM // 4
    inv_freq = ROPE_THETA ** (-jnp.arange(npair, dtype=F32) / npair)
    ang = jnp.concatenate([row[:, None] * inv_freq, col[:, None] * inv_freq], axis=-1)
    cos = jnp.tile(jnp.cos(ang), (1, 4))
    sin = jnp.tile(jnp.concatenate([-jnp.sin(ang), jnp.sin(ang)], axis=-1), (1, 2))
    cos = jnp.concatenate([jnp.ones((tm, LANES), F32), cos], axis=0)
    sin = jnp.concatenate([jnp.zeros((tm, LANES), F32), sin], axis=0)
    return cos, sin


def _tile(nc, t, cap):
    for cand in (1024, 512, 256, 128):
        if cand <= cap and nc % cand == 0 and t % cand == 0:
            return cand
    raise ValueError("unsupported sequence lengths")


def kernel(x, c, ctx, c_ctx, w_ada, b_ada, w_in, b_in, q_norm, k_norm, gmlp_norm, gmlp_ws, gmlp_b, mix_norm, w_out,
           router_w, router_b, w_gate_up, b_gate_up, w_down, b_down, final_norm):
    b, t, d = x.shape
    tc = ctx.shape[1]
    depth = w_in.shape[0]
    nc, nl = b * tc, b * t
    n = nc + nl
    tm = _tile(nc, t, 512)
    rs = _tile(tc, t, 256)
    tq = _tile(tc, t, 256)
    tk = _tile(t, t, 512)
    tmb = 512
    tt = _tile(nc, t, 256)
    nct, tpb = nc // tm, t // tm
    n_blocks = -(-n * TOP_K // tmb) + N_EXPERTS

    def heads_0213(a, axis):
        blk = [lax.slice_in_dim(a, i * HEAD_DIM, (i + 1) * HEAD_DIM, axis=axis) for i in (0, 2, 1, 3)]
        rest = lax.slice_in_dim(a, MLSTM_HEADS * HEAD_DIM, a.shape[axis], axis=axis)
        return jnp.concatenate(blk + [rest], axis=axis)

    def cols(a):
        pad = jnp.zeros(a.shape[:-1] + (LANES - 16,), a.dtype)
        return jnp.concatenate([a[..., 0:768], heads_0213(a[..., 768:1024], a.ndim - 1), a[..., 1040:2320],
                                a[..., 1024:1040], pad], axis=-1)

    w_in_r = cols(w_in).astype(BF16)
    b_in_r = cols(b_in).reshape(depth, 1, C_END)
    qn = jnp.tile(q_norm, (1, ATTN_HEADS)).reshape(depth, 1, 512)
    kn = jnp.tile(k_norm, (1, 2)).reshape(depth, 1, LANES)
    gn = gmlp_norm.reshape(depth, 1, 256)
    ws4 = gmlp_ws.reshape(depth, GMLP_GROUPS * GMLP_CHUNK, GMLP_CHUNK).astype(BF16)
    gb = jnp.repeat(jnp.swapaxes(gmlp_b, 1, 2), HEAD_DIM, axis=2)
    mn = heads_0213(mix_norm, 1).reshape(depth, 1, d)
    wo = heads_0213(w_out, 1).astype(BF16)
    rw = jnp.pad(router_w, ((0, 0), (0, 0), (0, LANES - N_EXPERTS)))
    rwh = rw.astype(BF16)
    rwl = (rw - rwh.astype(F32)).astype(BF16)
    rb = jnp.pad(router_b, ((0, 0), (0, LANES - N_EXPERTS))).reshape(depth, 1, LANES)
    wgl = _wprep_call(w_gate_up)
    wd = w_down.astype(BF16)
    ne, ff = w_down.shape[1], w_down.shape[2]
    bgl = jnp.concatenate([b_gate_up[..., 0::2], b_gate_up[..., 1::2]], axis=-1).reshape(depth, ne, 1, 2 * ff)
    bd = b_down.reshape(depth, ne, 1, d)
    cos_t, sin_t = _rope_tables(t, tm)

    r = -(-(b + 1) // 8) * 8
    cs = jnp.zeros((r, d), F32).at[0].set(c_ctx).at[1:b + 1].set(c)
    mods = _ada_call(cs, w_ada, b_ada).reshape(depth, r, 6, d)

    xs = jnp.concatenate([ctx.reshape(nc, d), x.reshape(nl, d)], axis=0)
    for l in range(depth):
        qtd, mk, vt2, gt3, mo, g, gm, aq, akd, avd = _proj_call(
            xs, mods, w_in_r, b_in_r, qn, kn, gn, ws4, gb, cos_t, sin_t, l=l, tm=tm, nct=nct, tpb=tpb)
        hf, hb = _mlstm_call(mk, qtd, vt2, g, gt3, b=b, rs=rs, nsc=tc // rs, nsl=t // rs, ncb=nc // rs)
        a = _attn_call(aq, akd, avd, b=b, tq=tq, tc=tc, t=t, tk=tk, ncq=nc // tq)
        xs, h2, logits = _merge_call(hf, hb, mo, gm, a, xs, mods, mn, wo, rwh, rwl, rb, l=l, tm=tm, nct=nct, tpb=tpb)
        rn, rt, cnt = _route_call(logits, tr=tm)
        dest_t, slot_tok, block_expert, nvalid = _slots(rt, cnt, tmb, n_blocks)
        y_slots = _moe_call(block_expert, nvalid, slot_tok.reshape(n_blocks, 1, tmb),
                            h2.reshape(n, d // LANES, LANES), wgl, wd, bgl, bd, l=l, tmb=tmb)
        dest_blk = dest_t.reshape(TOP_K, n // tt, tt).transpose(1, 0, 2).reshape(n // tt, 1, TOP_K * tt)
        xs = _combine_call(dest_blk, y_slots.reshape(n_blocks * tmb, d // LANES, LANES), rn, xs, mods, l=l, tt=tt,
                           nct=nc // tt, tpb=t // tt)
    out = _final_call(xs, final_norm, tm=tm, nct=nct, nl=nl)
    return out.reshape(b, t, d)
```

```python
import functools

import jax
import jax.numpy as jnp
from jax import lax
from jax.experimental import pallas as pl
from jax.experimental.pallas import tpu as pltpu

F32 = jnp.float32
BF16 = jnp.bfloat16

HEAD_DIM = 64
EPS = 1e-6
GRID_W = 64
ROPE_THETA = 10000.0
MLSTM_HEADS = 4
MLSTM_CHUNK = 64
GMLP_GROUPS = 4
GMLP_CHUNK = 128
ATTN_HEADS = 8
ATTN_KV_HEADS = 2
N_EXPERTS = 32
TOP_K = 4
SWIGLU_LIMIT = 7.0
SWIGLU_ALPHA = 1.702
LANES = 128
NEG_BIG = -1e30

C_MQ, C_MK, C_MV, C_MO, C_GU, C_GV, C_AQ, C_AK, C_AV, C_G, C_END = (
    0, 256, 512, 768, 1024, 1280, 1536, 2048, 2176, 2304, 2432)


def _dot(a, b):
    return jnp.dot(a, b, preferred_element_type=F32)


def _dot_nt(a, b):
    return lax.dot_general(a, b, (((1,), (1,)), ((), ())), preferred_element_type=F32)


def _split(x):
    hi = x.astype(BF16)
    lo = (x - hi.astype(F32)).astype(BF16)
    return hi, lo


def _cparams(ndims, vmem_mb):
    return pltpu.CompilerParams(dimension_semantics=("arbitrary",) * ndims,
                                vmem_limit_bytes=vmem_mb << 20)


def _sigmoid(x):
    return 1.0 / (1.0 + jnp.exp(-x))


def _gelu_tanh(x):
    return 0.5 * x * (1.0 + jnp.tanh(0.7978845608028654 * (x + 0.044715 * (x * x * x))))


def _rms(x):
    return x * lax.rsqrt(jnp.mean(x * x, axis=-1, keepdims=True) + EPS)


def _seg_indicator(n):
    r = lax.broadcasted_iota(jnp.int32, (n, n), 0) // HEAD_DIM
    c = lax.broadcasted_iota(jnp.int32, (n, n), 1) // HEAD_DIM
    return jnp.where(r == c, 1.0, 0.0).astype(BF16)


def _head_rms(x, ind):
    hi, lo = _split(x * x)
    ss = _dot(hi, ind) + _dot(lo, ind)
    return x * lax.rsqrt(ss * (1.0 / HEAD_DIM) + EPS)


def _ada_kernel(c_ref, w_ref, b_ref, o_ref):
    c = c_ref[...]
    s = c * _sigmoid(c)
    s_hi, s_lo = _split(s)
    w_hi, w_lo = _split(w_ref[0])
    o_ref[0] = _dot(s_hi, w_hi) + _dot(s_lo, w_hi) + _dot(s_hi, w_lo) + b_ref[0]


def _ada_call(cs, w_ada, b_ada):
    depth, d, d6 = w_ada.shape
    r = cs.shape[0]
    tn = 1536
    return pl.pallas_call(
        _ada_kernel,
        grid=(depth, d6 // tn),
        in_specs=[pl.BlockSpec((r, d), lambda l, j: (0, 0)),
                  pl.BlockSpec((1, d, tn), lambda l, j: (l, 0, j)),
                  pl.BlockSpec((1, 1, tn), lambda l, j: (l, 0, j))],
        out_specs=pl.BlockSpec((1, r, tn), lambda l, j: (l, 0, j)),
        out_shape=jax.ShapeDtypeStruct((depth, r, d6), F32),
        compiler_params=_cparams(2, 40),
        name="ada",
    )(cs, w_ada, b_ada.reshape(depth, 1, d6))


def _proj_kernel(x_ref, mod_ref, w_ref, b_ref, qn_ref, kn_ref, gn_ref, ws_ref, gb_ref, cos_ref, sin_ref,
                 qt_ref, mk_ref, vt_ref, gt_ref, mo_ref, g_ref, gm_ref, aq_ref, ak_ref, av_ref, *, tm):
    h = (_rms(x_ref[...]) * (1.0 + mod_ref[0, 1:2, :]) + mod_ref[0, 0:1, :]).astype(BF16)

    def seg(a, b):
        return _dot(h, w_ref[:, a:b]) + b_ref[:, a:b]

    lo_q = lax.broadcasted_iota(jnp.int32, (2 * LANES, LANES), 1) < MLSTM_CHUNK
    lo_v = lax.broadcasted_iota(jnp.int32, (HEAD_DIM, LANES), 1) < MLSTM_CHUNK
    qt = seg(C_MQ, C_MK).T
    vt = seg(C_MV, C_MO).T
    for m in range(tm // LANES):
        even, odd = slice(2 * m * LANES, (2 * m + 1) * LANES), slice((2 * m + 1) * LANES, (2 * m + 2) * LANES)
        qv = qt[:, m * LANES:(m + 1) * LANES]
        qr = pltpu.roll(qv, MLSTM_CHUNK, 1)
        qt_ref[:, even] = jnp.where(lo_q, qv, qr).astype(BF16)
        qt_ref[:, odd] = jnp.where(lo_q, qr, qv).astype(BF16)
        for p in range(MLSTM_HEADS // 2):
            va = vt[p * LANES:p * LANES + HEAD_DIM, m * LANES:(m + 1) * LANES]
            vb = vt[p * LANES + HEAD_DIM:(p + 1) * LANES, m * LANES:(m + 1) * LANES]
            vt_ref[p, :, even] = jnp.where(lo_v, va, pltpu.roll(vb, MLSTM_CHUNK, 1)).astype(BF16)
            vt_ref[p, :, odd] = jnp.where(lo_v, pltpu.roll(va, MLSTM_CHUNK, 1), vb).astype(BF16)
    mk_ref[...] = (seg(C_MK, C_MV) * HEAD_DIM ** -0.5).astype(BF16)
    mo_ref[...] = seg(C_MO, C_GU)

    g = seg(C_G, C_END)
    ls = jnp.minimum(g, 0.0) - jnp.log1p(jnp.exp(-jnp.abs(g)))
    r = lax.broadcasted_iota(jnp.int32, (tm, tm), 0)
    c = lax.broadcasted_iota(jnp.int32, (tm, tm), 1)
    same = (r // MLSTM_CHUNK) == (c // MLSTM_CHUNK)
    pre = jnp.where(same, jnp.where(c <= r, 1.0, 0.0), 0.0).astype(BF16)
    suf = jnp.where(same, jnp.where(c >= r, 1.0, 0.0), 0.0).astype(BF16)
    ls_hi, ls_lo = _split(ls)
    psum = _dot(pre, ls_hi) + _dot(pre, ls_lo)
    ssum = _dot(suf, ls_hi) + _dot(suf, ls_lo)
    lane = lax.broadcasted_iota(jnp.int32, (tm, LANES), 1)
    q4 = lane // MLSTM_HEADS
    gates = jnp.where(q4 == 1, psum, jnp.where(q4 == 3, ssum, g))
    g_ref[...] = gates
    gt = gates.T
    cols_a = [dd * 2 * MLSTM_HEADS + qq * MLSTM_HEADS + 2 * pp for dd in range(2) for pp in range(2) for qq in range(2)]
    ra = jnp.concatenate([gt[ca:ca + 1, :] for ca in cols_a], axis=0)
    rb = jnp.concatenate([gt[ca + 1:ca + 2, :] for ca in cols_a], axis=0)
    lo_g = lax.broadcasted_iota(jnp.int32, (8, LANES), 1) < MLSTM_CHUNK
    for m in range(tm // LANES):
        am, bm = ra[:, m * LANES:(m + 1) * LANES], rb[:, m * LANES:(m + 1) * LANES]
        gt_ref[:, 2 * m * LANES:(2 * m + 1) * LANES] = jnp.where(lo_g, am, pltpu.roll(bm, MLSTM_CHUNK, 1))
        gt_ref[:, (2 * m + 1) * LANES:(2 * m + 2) * LANES] = jnp.where(lo_g, pltpu.roll(am, MLSTM_CHUNK, 1), bm)

    u = _gelu_tanh(seg(C_GU, C_GV))
    v = (_rms(_gelu_tanh(seg(C_GV, C_AQ))) * gn_ref[...]).astype(BF16)
    grp = lax.broadcasted_iota(jnp.int32, (GMLP_CHUNK, 2 * LANES), 1) // HEAD_DIM
    for ci in range(tm // GMLP_CHUNK):
        rows = slice(ci * GMLP_CHUNK, (ci + 1) * GMLP_CHUNK)
        full = _dot(ws_ref[...], v[rows, :])
        mixed = gb_ref[...]
        for gi in range(GMLP_GROUPS):
            mixed = mixed + jnp.where(grp == gi, full[gi * GMLP_CHUNK:(gi + 1) * GMLP_CHUNK, :], 0.0)
        gm_ref[rows, :] = u[rows, :] * mixed

    cos = cos_ref[...]
    sin = sin_ref[...]
    first_half = (lane % HEAD_DIM) < (HEAD_DIM // 2)
    lo_half = lane < HEAD_DIM

    def rope(blk):
        partner = jnp.where(first_half, pltpu.roll(blk, LANES - HEAD_DIM // 2, 1),
                            pltpu.roll(blk, HEAD_DIM // 2, 1))
        return blk * cos + partner * sin

    ind256 = _seg_indicator(2 * LANES)
    for s in range(2):
        qn = _head_rms(seg(C_AQ + s * 256, C_AQ + (s + 1) * 256), ind256) * qn_ref[:, s * 256:(s + 1) * 256]
        for j in range(2):
            blk = rope(qn[:, j * LANES:(j + 1) * LANES]) * HEAD_DIM ** -0.5
            aq_ref[:, s * 256 + j * LANES: s * 256 + (j + 1) * LANES] = blk.astype(BF16)

    kr = rope(_head_rms(seg(C_AK, C_AV), ind256[:LANES, :LANES]) * kn_ref[...])
    kroll = pltpu.roll(kr, HEAD_DIM, 1)
    ak_ref[:, 0:LANES] = jnp.where(lo_half, kr, kroll).astype(BF16)
    ak_ref[:, LANES:2 * LANES] = jnp.where(lo_half, kroll, kr).astype(BF16)

    vf = seg(C_AV, C_G)
    vroll = pltpu.roll(vf, HEAD_DIM, 1)
    av_ref[:, 0:LANES] = jnp.where(lo_half, vf, 1.0).astype(BF16)
    av_ref[:, LANES:2 * LANES] = jnp.where(lo_half, 1.0, vroll).astype(BF16)
    av_ref[:, 2 * LANES:3 * LANES] = jnp.where(lo_half, vroll, 1.0).astype(BF16)
    av_ref[:, 3 * LANES:4 * LANES] = jnp.where(lo_half, 1.0, vf).astype(BF16)


def _proj_call(x, mods, w, b, qn, kn, gn, ws4, gb, cos_t, sin_t, *, l, tm, nct, tpb):
    n, d = x.shape

    def mod_idx(i):
        return (l, jnp.where(i < nct, 0, 1 + (i - nct) // tpb), 0, 0)

    def rope_idx(i):
        return (jnp.where(i < nct, 0, 1 + (i - nct) % tpb), 0)

    row = lambda i: (i, 0)
    const2 = lambda i: (0, 0)
    lay = lambda i: (l, 0, 0)
    nlane = tm // MLSTM_CHUNK * LANES
    outs = [(256, BF16), (256, F32), (LANES, F32), (256, F32), (512, BF16), (256, BF16), (512, BF16)]
    row_specs = [pl.BlockSpec((tm, wd), row) for wd, _ in outs]
    row_shapes = [jax.ShapeDtypeStruct((n, wd), dt) for wd, dt in outs]
    qt_spec = pl.BlockSpec((256, nlane), lambda i: (0, i))
    vt_spec = pl.BlockSpec((MLSTM_HEADS // 2, HEAD_DIM, nlane), lambda i: (0, 0, i))
    gt_spec = pl.BlockSpec((8, nlane), lambda i: (0, i))
    gt_shape = jax.ShapeDtypeStruct((8, n // MLSTM_CHUNK * LANES), F32)
    qt_shape = jax.ShapeDtypeStruct((256, n // MLSTM_CHUNK * LANES), BF16)
    vt_shape = jax.ShapeDtypeStruct((MLSTM_HEADS // 2, HEAD_DIM, n // MLSTM_CHUNK * LANES), BF16)
    return pl.pallas_call(
        functools.partial(_proj_kernel, tm=tm),
        grid=(n // tm,),
        in_specs=[pl.BlockSpec((tm, d), row),
                  pl.BlockSpec((None, 1, 6, d), mod_idx),
                  pl.BlockSpec((None, d, C_END), lay),
                  pl.BlockSpec((None, 1, C_END), lay),
                  pl.BlockSpec((None, 1, 512), lay),
                  pl.BlockSpec((None, 1, LANES), lay),
                  pl.BlockSpec((None, 1, 256), lay),
                  pl.BlockSpec((None, GMLP_GROUPS * GMLP_CHUNK, GMLP_CHUNK), lay),
                  pl.BlockSpec((None, GMLP_CHUNK, 256), lay),
                  pl.BlockSpec((tm, LANES), rope_idx),
                  pl.BlockSpec((tm, LANES), rope_idx)],
        out_specs=[qt_spec, row_specs[0], vt_spec, gt_spec] + row_specs[1:],
        out_shape=[qt_shape, row_shapes[0], vt_shape, gt_shape] + row_shapes[1:],
        compiler_params=_cparams(1, 48),
        name="proj",
    )(x, mods, w, b, qn, kn, gn, ws4, gb, cos_t, sin_t)


def _mlstm_kernel(kf_ref, qf_ref, vf_ref, gf_ref, gtf_ref, kb_ref, qb_ref, vb_ref, gb_ref, gtb_ref,
                  hf_ref, hb_ref, c_ref, m_ref, *, nch):
    ch = MLSTM_CHUNK

    @pl.when(pl.program_id(1) == 0)
    def _():
        c_ref[...] = jnp.zeros_like(c_ref)
        m_ref[...] = jnp.zeros_like(m_ref)

    sub = lax.broadcasted_iota(jnp.int32, (ch, LANES), 0)
    lane = lax.broadcasted_iota(jnp.int32, (ch, LANES), 1)
    lo_half = lane < ch
    pos = lane % ch
    r2 = lax.broadcasted_iota(jnp.int32, (LANES, LANES), 0)
    c2 = lax.broadcasted_iota(jnp.int32, (LANES, LANES), 1)
    blockdiag = (r2 < ch) == (c2 < ch)
    ones_rows = jnp.ones((ch, LANES), BF16)
    dirs = ((kf_ref, qf_ref, vf_ref, gf_ref, gtf_ref, hf_ref), (kb_ref, qb_ref, vb_ref, gb_ref, gtb_ref, hb_ref))
    for d, (k_ref, q_ref, v_ref, g_ref, gt_ref, h_ref) in enumerate(dirs):
        mask = (sub <= pos) if d == 0 else (sub >= pos)
        sels = []
        for p in range(MLSTM_HEADS // 2):
            gi = d * 2 * MLSTM_HEADS + 2 * p + jnp.where(c2 >= ch, 1, 0)
            sels.append((jnp.where(r2 == gi, 1.0, 0.0) - jnp.where(r2 == gi + MLSTM_HEADS, 1.0, 0.0)).astype(BF16))
        order = range(nch) if d == 0 else range(nch - 1, -1, -1)
        for ci in order:
            rows = slice(ci * ch, (ci + 1) * ch)
            cl = slice(ci * LANES, (ci + 1) * LANES)
            g = g_ref[rows, :]
            g_hi = g.astype(BF16)
            g_r = g - g_hi.astype(F32)
            g_mid = g_r.astype(BF16)
            g_lo = (g_r - g_mid.astype(F32)).astype(BF16)
            hts = []
            for p in range(MLSTM_HEADS // 2):
                st = d * (MLSTM_HEADS // 2) + p
                kp = k_ref[rows, p * LANES:(p + 1) * LANES]
                zero = jnp.zeros_like(kp)
                qbd = jnp.where(blockdiag, q_ref[p * LANES:(p + 1) * LANES, cl], jnp.zeros((LANES, LANES), BF16))
                vat = jnp.concatenate([v_ref[p, :, cl], ones_rows], axis=0)
                irow = gt_ref[2 * st:2 * st + 1, cl]
                brow = gt_ref[2 * st + 1:2 * st + 2, cl]
                ea, eb = (ch - 1, 2 * ch - 1) if d == 0 else (0, ch)
                bend = jnp.where(lo_half[0:1], brow[:, ea:ea + 1], brow[:, eb:eb + 1])
                x = _dot(g_hi, sels[p]) + _dot(g_mid, sels[p]) + _dot(g_lo, sels[p])
                dmt = jnp.where(mask, brow + x, -jnp.inf)
                cmax = jnp.max(dmt, axis=0, keepdims=True)
                p0 = (_dot(kp, qbd) * jnp.exp(dmt - cmax)).astype(BF16)
                rt0 = _dot(vat, jnp.concatenate([jnp.where(lo_half, p0, zero), jnp.where(lo_half, zero, p0)], axis=0))
                dmax = jnp.max(bend + x, axis=0, keepdims=True)
                vw = (vat.astype(F32) * jnp.exp(bend + irow - brow - dmax)).astype(BF16)
                u0 = _dot(vw, jnp.concatenate([jnp.where(lo_half, kp, zero), jnp.where(lo_half, zero, kp)], axis=0))
                m = m_ref[st][0:1, :]
                cst = c_ref[st]
                inter = brow + m
                mt = jnp.maximum(inter, cmax)
                rt = jnp.exp(cmax - mt) * rt0 + jnp.exp(inter - mt) * _dot(cst.astype(BF16), qbd)
                hts.append(rt[0:ch] / jnp.maximum(jnp.abs(rt[ch:2 * ch]), jnp.exp(-mt)))
                mnew = jnp.maximum(bend + m, dmax)
                c_ref[st] = jnp.exp(bend + m - mnew) * cst + jnp.exp(dmax - mnew) * u0
                m_ref[st] = jnp.broadcast_to(mnew, (8, LANES))
            ht = jnp.concatenate(hts, axis=0).T
            h_ref[rows, 0:LANES] = ht[0:ch]
            h_ref[rows, LANES:2 * LANES] = ht[ch:2 * ch]


def _mlstm_call(mk, qtd, vt2, g, gt3, *, b, rs, nsc, nsl, ncb):
    n = mk.shape[0]
    nch = rs // MLSTM_CHUNK

    def fwd(bi, j):
        return jnp.where(j < nsc, bi * nsc + j, ncb + bi * nsl + (j - nsc))

    def bwd(bi, j):
        return jnp.where(j < nsc, bi * nsc + (nsc - 1 - j), ncb + bi * nsl + (nsl - 1 - (j - nsc)))

    def specs(idx):
        return [pl.BlockSpec((rs, 256), lambda bi, j: (idx(bi, j), 0)),
                pl.BlockSpec((256, nch * LANES), lambda bi, j: (0, idx(bi, j))),
                pl.BlockSpec((2, HEAD_DIM, nch * LANES), lambda bi, j: (0, 0, idx(bi, j))),
                pl.BlockSpec((rs, LANES), lambda bi, j: (idx(bi, j), 0)),
                pl.BlockSpec((8, nch * LANES), lambda bi, j: (0, idx(bi, j)))]

    return pl.pallas_call(
        functools.partial(_mlstm_kernel, nch=nch),
        grid=(b, nsc + nsl),
        in_specs=specs(fwd) + specs(bwd),
        out_specs=[pl.BlockSpec((rs, 256), lambda bi, j: (fwd(bi, j), 0)),
                   pl.BlockSpec((rs, 256), lambda bi, j: (bwd(bi, j), 0))],
        out_shape=[jax.ShapeDtypeStruct((n, 256), F32)] * 2,
        scratch_shapes=[pltpu.VMEM((MLSTM_HEADS, LANES, LANES), F32),
                        pltpu.VMEM((MLSTM_HEADS, 8, LANES), F32)],
        compiler_params=_cparams(2, 32),
        name="mlstm",
    )(mk, qtd, vt2, g, gt3, mk, qtd, vt2, g, gt3)


def _attn_kernel(q_ref, kc_ref, kl_ref, vc_ref, vl_ref, o_ref, qs_ref, m_ref, aa_ref, ab_ref, *, tq, tc, t, tk, nqc):
    lane = lax.broadcasted_iota(jnp.int32, (tq, LANES), 1)
    lo_half = lane < HEAD_DIM
    for g in range(ATTN_KV_HEADS):
        for pp in range(2):
            qp = q_ref[:, (2 * g + pp) * LANES:(2 * g + pp + 1) * LANES]
            zero = jnp.zeros_like(qp)
            qs_ref[g, pp * tq:(pp + 1) * tq, :] = jnp.where(lo_half, qp, zero)
            qs_ref[g, (2 + pp) * tq:(3 + pp) * tq, :] = jnp.where(lo_half, zero, qp)
    m_ref[...] = jnp.full_like(m_ref, NEG_BIG)
    aa_ref[...] = jnp.zeros_like(aa_ref)
    ab_ref[...] = jnp.zeros_like(ab_ref)

    def chunk(k_ref, v_ref, start, size):
        for g in range(ATTN_KV_HEADS):
            kc = k_ref[pl.ds(start, size), g * LANES:(g + 1) * LANES]
            s = _dot_nt(qs_ref[g], kc)
            m_old = m_ref[g]
            m_new = jnp.maximum(m_old, jnp.max(s, axis=-1, keepdims=True))
            alpha = jnp.exp(m_old - m_new)
            p = jnp.exp(s - jnp.concatenate([m_new] * (size // LANES), axis=1)).astype(BF16)
            va = v_ref[pl.ds(start, size), (2 * g) * LANES:(2 * g + 1) * LANES]
            vb = v_ref[pl.ds(start, size), (2 * g + 1) * LANES:(2 * g + 2) * LANES]
            aa_ref[g] = alpha[:2 * tq] * aa_ref[g] + _dot(p[:2 * tq], va)
            ab_ref[g] = alpha[2 * tq:] * ab_ref[g] + _dot(p[2 * tq:], vb)
            m_ref[g] = m_new

    for c0 in range(0, tc, tk):
        chunk(kc_ref, vc_ref, c0, min(tk, tc - c0))

    @pl.when(pl.program_id(1) >= nqc)
    def _():
        def body(i, carry):
            chunk(kl_ref, vl_ref, pl.multiple_of(i * tk, LANES), tk)
            return carry
        lax.fori_loop(0, t // tk, body, 0, unroll=2)

    for g in range(ATTN_KV_HEADS):
        for pp in range(2):
            a = aa_ref[g, pp * tq:(pp + 1) * tq, :]
            bq = ab_ref[g, pp * tq:(pp + 1) * tq, :]
            oa = a * pltpu.roll(1.0 / a, HEAD_DIM, 1)
            ob = bq * pltpu.roll(1.0 / bq, HEAD_DIM, 1)
            o_ref[:, (2 * g + pp) * LANES:(2 * g + pp + 1) * LANES] = jnp.where(lo_half, oa, ob)


def _attn_call(aq, akd, avd, *, b, tq, tc, t, tk, ncq):
    n = aq.shape[0]
    nqc, nql = tc // tq, t // tq
    nc = b * tc
    assert nc % t == 0

    def qidx(bi, j):
        return (jnp.where(j < nqc, bi * nqc + j, ncq + bi * nql + (j - nqc)), 0)

    return pl.pallas_call(
        functools.partial(_attn_kernel, tq=tq, tc=tc, t=t, tk=tk, nqc=nqc),
        grid=(b, nqc + nql),
        in_specs=[pl.BlockSpec((tq, 512), qidx),
                  pl.BlockSpec((tc, 256), lambda bi, j: (bi, 0)),
                  pl.BlockSpec((t, 256), lambda bi, j: (nc // t + bi, 0)),
                  pl.BlockSpec((tc, 512), lambda bi, j: (bi, 0)),
                  pl.BlockSpec((t, 512), lambda bi, j: (nc // t + bi, 0))],
        out_specs=pl.BlockSpec((tq, 512), qidx),
        out_shape=jax.ShapeDtypeStruct((n, 512), F32),
        scratch_shapes=[pltpu.VMEM((ATTN_KV_HEADS, 4 * tq, LANES), BF16),
                        pltpu.VMEM((ATTN_KV_HEADS, 4 * tq, LANES), F32),
                        pltpu.VMEM((ATTN_KV_HEADS, 2 * tq, LANES), F32),
                        pltpu.VMEM((ATTN_KV_HEADS, 2 * tq, LANES), F32)],
        compiler_params=_cparams(2, 48),
        name="attn",
    )(aq, akd, akd, avd, avd)


def _merge_kernel(hf_ref, hb_ref, mo_ref, gm_ref, a_ref, x_ref, mod_ref, mn_ref, wo_ref, rwh_ref, rwl_ref, rb_ref,
                  xo_ref, h2_ref, lg_ref):
    ind = _seg_indicator(2 * LANES)
    parts = [_sigmoid(mo_ref[...]) * (hf_ref[...] + hb_ref[...]), gm_ref[...],
             a_ref[:, 0:256], a_ref[:, 256:512]]
    y = jnp.concatenate([_head_rms(pt, ind) for pt in parts], axis=1) * mn_ref[...]
    xn = x_ref[...] + mod_ref[0, 2:3, :] * _dot(y.astype(BF16), wo_ref[...])
    xo_ref[...] = xn
    h2 = _rms(xn) * (1.0 + mod_ref[0, 4:5, :]) + mod_ref[0, 3:4, :]
    nsl = h2.shape[1] // LANES
    for sl in range(nsl):
        h2_ref[pl.ds(sl, h2.shape[0], stride=nsl), :] = h2[:, sl * LANES:(sl + 1) * LANES]
    hi, lo = _split(h2)
    lg_ref[...] = _dot(hi, rwh_ref[...]) + _dot(lo, rwh_ref[...]) + _dot(hi, rwl_ref[...]) + rb_ref[...]


def _merge_call(hf, hb, mo, gm, a, x, mods, mn, wo, rwh, rwl, rb, *, l, tm, nct, tpb):
    n, d = x.shape

    def mod_idx(i):
        return (l, jnp.where(i < nct, 0, 1 + (i - nct) // tpb), 0, 0)

    row = lambda i: (i, 0)
    lay = lambda i: (l, 0, 0)
    return pl.pallas_call(
        _merge_kernel,
        grid=(n // tm,),
        in_specs=[pl.BlockSpec((tm, 256), row), pl.BlockSpec((tm, 256), row), pl.BlockSpec((tm, 256), row),
                  pl.BlockSpec((tm, 256), row), pl.BlockSpec((tm, 512), row), pl.BlockSpec((tm, d), row),
                  pl.BlockSpec((None, 1, 6, d), mod_idx),
                  pl.BlockSpec((None, 1, d), lay),
                  pl.BlockSpec((None, d, d), lay),
                  pl.BlockSpec((None, d, LANES), lay),
                  pl.BlockSpec((None, d, LANES), lay),
                  pl.BlockSpec((None, 1, LANES), lay)],
        out_specs=[pl.BlockSpec((tm, d), row), pl.BlockSpec((tm * (d // LANES), LANES), row),
                   pl.BlockSpec((tm, LANES), row)],
        out_shape=[jax.ShapeDtypeStruct((n, d), F32), jax.ShapeDtypeStruct((n * (d // LANES), LANES), F32),
                   jax.ShapeDtypeStruct((n, LANES), F32)],
        compiler_params=_cparams(1, 48),
        name="merge",
    )(hf, hb, mo, gm, a, x, mods, mn, wo, rwh, rwl, rb)


def _wprep_kernel(w_ref, o_ref, *, f):
    slab = 2 * LANES
    r = lax.broadcasted_iota(jnp.int32, (slab, slab), 0)
    c = lax.broadcasted_iota(jnp.int32, (slab, slab), 1)
    perm = jnp.where(r == jnp.where(c < LANES, 2 * c, 2 * (c - LANES) + 1), 1.0, 0.0).astype(BF16)
    for s in range(2 * f // slab):
        y = _dot(w_ref[:, s * slab:(s + 1) * slab].astype(BF16), perm).astype(BF16)
        o_ref[:, s * LANES:(s + 1) * LANES] = y[:, :LANES]
        o_ref[:, f + s * LANES:f + (s + 1) * LANES] = y[:, LANES:]


def _wprep_call(w_gate_up):
    depth, ne, d, f2 = w_gate_up.shape
    rows = 512
    w2 = w_gate_up.reshape(depth * ne * d, f2)
    out = pl.pallas_call(
        functools.partial(_wprep_kernel, f=f2 // 2),
        grid=(depth * ne * d // rows,),
        in_specs=[pl.BlockSpec((rows, f2), lambda i: (i, 0))],
        out_specs=pl.BlockSpec((rows, f2), lambda i: (i, 0)),
        out_shape=jax.ShapeDtypeStruct((depth * ne * d, f2), BF16),
        compiler_params=_cparams(1, 32),
        name="wprep",
    )(w2)
    return out.reshape(depth, ne, d, f2)


def _moe_kernel(be_ref, nv_ref, tok0_ref, tok1_ref, tokn_ref, h_hbm, wgl_ref, wd_ref, bgl_ref, bd_ref, y_ref,
                xbuf, act_ref, sem, *, tmb, f):
    i = pl.program_id(0)
    nv = nv_ref[0]
    nbuf = xbuf.shape[0]
    slot = i % nbuf
    cw = 2 * LANES
    nchunk = f // cw
    per = tmb // (2 * nchunk)
    nsl = h_hbm.shape[1]

    def request(tok_ref, r, dst, priority=0):
        pltpu.make_async_copy(h_hbm.at[tok_ref[0, 0, r]], xbuf.at[dst, pl.ds(pl.multiple_of(r * nsl, nsl), nsl)],
                              sem.at[dst]).start(priority=priority)

    @pl.when(i == 0)
    def _():
        for dst, tok_ref in enumerate((tok0_ref, tok1_ref)):
            def issue(r, carry):
                request(tok_ref, r, dst)
                return carry
            lax.fori_loop(0, tmb, issue, 0, unroll=8)

    @pl.when(i <= nv + 1)
    def _():
        pltpu.make_async_copy(xbuf.at[slot], xbuf.at[slot], sem.at[slot]).wait()

    def compute(cur):
        nxt = (cur + 2) % nbuf
        xb = jnp.concatenate([xbuf[cur, pl.ds(sl, tmb, stride=nsl), :] for sl in range(nsl)], axis=1).astype(BF16)
        for j in range(nchunk):
            cols = slice(j * cw, (j + 1) * cw)
            ucols = slice(f + j * cw, f + (j + 1) * cw)
            x_glu = jnp.minimum(_dot(xb, wgl_ref[:, cols]) + bgl_ref[:, cols], SWIGLU_LIMIT)
            for r in range(2 * j * per, (2 * j + 1) * per):
                request(tokn_ref, r, nxt, r % 2)
            x_lin = jnp.clip(_dot(xb, wgl_ref[:, ucols]) + bgl_ref[:, ucols], -SWIGLU_LIMIT, SWIGLU_LIMIT)
            for r in range((2 * j + 1) * per, (2 * j + 2) * per):
                request(tokn_ref, r, nxt, r % 2)
            act_ref[:, cols] = (x_glu * _sigmoid(SWIGLU_ALPHA * x_glu) * (x_lin + 1.0)).astype(BF16)
        act = act_ref[...]
        for j in range(wd_ref.shape[1] // cw):
            cols = slice(j * cw, (j + 1) * cw)
            y = _dot(act, wd_ref[:, cols]) + bd_ref[:, cols]
            for sl in range(cw // LANES):
                y_ref[pl.ds(j * (cw // LANES) + sl, tmb, stride=nsl), :] = y[:, sl * LANES:(sl + 1) * LANES]

    for cur in range(nbuf):
        pl.when(jnp.logical_and(i < nv, slot == cur))(functools.partial(compute, cur))

    @pl.when(i >= nv)
    def _():
        y_ref[...] = jnp.zeros_like(y_ref)


def _moe_call(block_expert, nvalid, slot_tok, h2, wgl, wd, bgl, bd, *, l, tmb):
    d = wd.shape[-1]
    nb = block_expert.shape[0]
    f = wd.shape[-2]
    wmap = lambda i, be, nv: (l, be[i], 0, 0)
    grid_spec = pltpu.PrefetchScalarGridSpec(
        num_scalar_prefetch=2,
        grid=(nb,),
        in_specs=[pl.BlockSpec((1, 1, tmb), lambda i, be, nv: (0, 0, 0), memory_space=pltpu.SMEM),
                  pl.BlockSpec((1, 1, tmb), lambda i, be, nv: (1, 0, 0), memory_space=pltpu.SMEM),
                  pl.BlockSpec((1, 1, tmb), lambda i, be, nv: (jnp.minimum(i + 2, nb - 1), 0, 0),
                               memory_space=pltpu.SMEM),
                  pl.BlockSpec(memory_space=pl.ANY),
                  pl.BlockSpec((None, None, d, 2 * f), wmap),
                  pl.BlockSpec((None, None, f, d), wmap),
                  pl.BlockSpec((None, None, 1, 2 * f), wmap),
                  pl.BlockSpec((None, None, 1, d), wmap)],
        out_specs=pl.BlockSpec((tmb * (d // LANES), LANES), lambda i, be, nv: (i, 0)),
        scratch_shapes=[pltpu.VMEM((3, tmb * (d // LANES), LANES), F32), pltpu.VMEM((tmb, f), BF16),
                        pltpu.SemaphoreType.DMA((3,))],
    )
    return pl.pallas_call(
        functools.partial(_moe_kernel, tmb=tmb, f=f),
        grid_spec=grid_spec,
        out_shape=jax.ShapeDtypeStruct((nb * tmb * (d // LANES), LANES), F32),
        compiler_params=_cparams(1, 48),
        name="moe",
    )(block_expert, nvalid, slot_tok, slot_tok, slot_tok, h2, wgl, wd, bgl, bd)


def _combine_kernel(dest0_ref, destn_ref, y_hbm, gate_ref, x_ref, mod_ref, o_ref, buf, sem, *, tt):
    i = pl.program_id(0)
    slot = i % 2
    nsl = y_hbm.shape[1]

    def request_tile(dest_ref, dst):
        def issue(r, carry):
            for k in range(TOP_K):
                pltpu.make_async_copy(y_hbm.at[dest_ref[0, 0, k * tt + r]],
                                      buf.at[dst, k, pl.ds(pl.multiple_of(r * nsl, nsl), nsl)],
                                      sem.at[dst]).start(priority=k % 2)
            return carry
        lax.fori_loop(0, tt, issue, 0, unroll=4)

    @pl.when(i == 0)
    def _():
        request_tile(dest0_ref, 0)

    @pl.when(i + 1 < pl.num_programs(0))
    def _():
        request_tile(destn_ref, 1 - slot)

    for k in range(TOP_K):
        pltpu.make_async_copy(buf.at[slot, k], buf.at[slot, k], sem.at[slot]).wait()
    gate = gate_ref[...]
    f = None
    for k in range(TOP_K):
        rows = jnp.concatenate([buf[slot, k, pl.ds(sl, tt, stride=nsl), :] for sl in range(nsl)], axis=1)
        g = gate[:, 2 * TOP_K + k:2 * TOP_K + k + 1]
        f = g * rows if f is None else f + g * rows
    o_ref[...] = x_ref[...] + mod_ref[0, 5:6, :] * f


def _combine_call(dest, y_slots, gates, x, mods, *, l, tt, nct, tpb):
    n, d = x.shape
    ntile = n // tt

    def mod_idx(i):
        return (l, jnp.where(i < nct, 0, 1 + (i - nct) // tpb), 0, 0)

    row = lambda i: (i, 0)
    return pl.pallas_call(
        functools.partial(_combine_kernel, tt=tt),
        grid=(ntile,),
        in_specs=[pl.BlockSpec((1, 1, tt * TOP_K), lambda i: (0, 0, 0), memory_space=pltpu.SMEM),
                  pl.BlockSpec((1, 1, tt * TOP_K), lambda i: (jnp.minimum(i + 1, ntile - 1), 0, 0),
                               memory_space=pltpu.SMEM),
                  pl.BlockSpec(memory_space=pl.ANY),
                  pl.BlockSpec((tt, LANES), row),
                  pl.BlockSpec((tt, d), row),
                  pl.BlockSpec((None, 1, 6, d), mod_idx)],
        out_specs=pl.BlockSpec((tt, d), row),
        out_shape=jax.ShapeDtypeStruct((n, d), F32),
        scratch_shapes=[pltpu.VMEM((2, TOP_K, tt * (d // LANES), LANES), F32), pltpu.SemaphoreType.DMA((2,))],
        compiler_params=_cparams(1, 32),
        name="combine",
    )(dest, dest, y_slots, gates, x, mods)


def _final_kernel(x_ref, g_ref, o_ref):
    o_ref[...] = _rms(x_ref[...]) * g_ref[...]


def _final_call(x, gain, *, tm, nct, nl):
    d = x.shape[1]
    return pl.pallas_call(
        _final_kernel,
        grid=(nl // tm,),
        in_specs=[pl.BlockSpec((tm, d), lambda i: (nct + i, 0)), pl.BlockSpec((1, d), lambda i: (0, 0))],
        out_specs=pl.BlockSpec((tm, d), lambda i: (i, 0)),
        out_shape=jax.ShapeDtypeStruct((nl, d), F32),
        compiler_params=_cparams(1, 32),
        name="final_norm",
    )(x, gain.reshape(1, d))


def _route_kernel(lg_ref, rn_ref, rt_ref, cnt_ref, run_ref, *, tr):
    @pl.when(pl.program_id(0) == 0)
    def _():
        run_ref[...] = jnp.zeros_like(run_ref)

    lane = lax.broadcasted_iota(jnp.int32, (tr, LANES), 1).astype(F32)
    lg = jnp.where(lane < N_EXPERTS, lg_ref[...], -jnp.inf)
    onehot = jnp.zeros((tr, LANES), F32)
    vals, idxs = [], []
    for _ in range(TOP_K):
        m = jnp.max(lg, axis=-1, keepdims=True)
        idx = jnp.min(jnp.where(lg == m, lane, float(LANES)), axis=-1, keepdims=True)
        hit = lane == idx
        onehot = jnp.where(hit, 1.0, onehot)
        lg = jnp.where(hit, -jnp.inf, lg)
        vals.append(m)
        idxs.append(idx)
    es = [jnp.exp(v - vals[0]) for v in vals]
    tot = es[0] + es[1] + es[2] + es[3]
    gates = [e / tot for e in es]
    r = lax.broadcasted_iota(jnp.int32, (tr, tr), 0)
    c = lax.broadcasted_iota(jnp.int32, (tr, tr), 1)
    strict = jnp.where(c < r, 1.0, 0.0).astype(BF16)
    run = run_ref[0:1, :]
    before = run + _dot(strict, onehot.astype(BF16))
    ranks = [jnp.sum(jnp.where(lane == idx, before, 0.0), axis=-1, keepdims=True) for idx in idxs]
    run_ref[...] = jnp.broadcast_to(run + jnp.sum(onehot, axis=0, keepdims=True), run_ref.shape)
    cnt_ref[...] = run_ref[...]
    out = jnp.zeros((tr, LANES), F32)
    for j, col in enumerate(idxs + ranks + gates):
        out = jnp.where(lane == float(j), col, out)
    rn_ref[...] = out
    rt_ref[...] = out.T


def _route_call(logits, *, tr):
    n = logits.shape[0]
    return pl.pallas_call(
        functools.partial(_route_kernel, tr=tr),
        grid=(n // tr,),
        in_specs=[pl.BlockSpec((tr, LANES), lambda i: (i, 0))],
        out_specs=[pl.BlockSpec((tr, LANES), lambda i: (i, 0)), pl.BlockSpec((LANES, tr), lambda i: (0, i)),
                   pl.BlockSpec((8, LANES), lambda i: (0, 0))],
        out_shape=[jax.ShapeDtypeStruct((n, LANES), F32), jax.ShapeDtypeStruct((LANES, n), F32),
                   jax.ShapeDtypeStruct((8, LANES), F32)],
        scratch_shapes=[pltpu.VMEM((8, LANES), F32)],
        compiler_params=_cparams(1, 32),
        name="route",
    )(logits)


def _slots(rt, cnt, tmb, n_blocks):
    n = rt.shape[1]
    counts = cnt[0, :N_EXPERTS].astype(jnp.int32)
    padded = (counts + tmb - 1) // tmb * tmb
    ends = jnp.cumsum(padded)
    start = ends - padded
    idx_t = rt[0:TOP_K].astype(jnp.int32)
    start_sel = jnp.zeros_like(idx_t)
    for e in range(N_EXPERTS):
        start_sel = jnp.where(idx_t == e, start[e], start_sel)
    dest_t = start_sel + rt[TOP_K:2 * TOP_K].astype(jnp.int32)
    nvalid = (ends[-1] // tmb).astype(jnp.int32)
    blk = jnp.minimum(jnp.arange(n_blocks, dtype=jnp.int32), nvalid - 1)
    block_expert = jnp.minimum(jnp.sum(((blk * tmb)[:, None] >= ends[None, :]).astype(jnp.int32), axis=1), N_EXPERTS - 1)
    nfill = n_blocks * tmb - n * TOP_K
    pad = padded - counts
    padcum = jnp.cumsum(pad)
    j = jnp.arange(nfill, dtype=jnp.int32)
    fe = jnp.sum(j[:, None] >= padcum[None, :], axis=1)
    first_free = jnp.concatenate([ends - pad, ends[-1:]])
    fill_slot = first_free[fe] + j - jnp.concatenate([padcum - pad, padcum[-1:]])[fe]
    tok = jnp.broadcast_to(jnp.arange(n, dtype=jnp.int32)[None, :], (TOP_K, n)).reshape(-1)
    _, slot_tok = lax.sort((jnp.concatenate([dest_t.reshape(-1), fill_slot]).astype(jnp.int32),
                            jnp.concatenate([tok, jnp.zeros((nfill,), jnp.int32)])), num_keys=1)
    return dest_t, slot_tok, block_expert, nvalid.reshape(1)


def _rope_tables(t, tm):
    rows = t // GRID_W
    row = jnp.repeat(jnp.arange(rows, dtype=F32), GRID_W)
    col = jnp.tile(jnp.arange(GRID_W, dtype=F32), rows)
    npair = HEAD_DIM // 4
    inv_freq = ROPE_THETA ** (-jnp.arange(npair, dtype=F32) / npair)
    ang = jnp.concatenate([row[:, None] * inv_freq, col[:, None] * inv_freq], axis=-1)
    cos = jnp.tile(jnp.cos(ang), (1, 4))
    sin = jnp.tile(jnp.concatenate([-jnp.sin(ang), jnp.sin(ang)], axis=-1), (1, 2))
    cos = jnp.concatenate([jnp.ones((tm, LANES), F32), cos], axis=0)
    sin = jnp.concatenate([jnp.zeros((tm, LANES), F32), sin], axis=0)
    return cos, sin


def _tile(nc, t, cap):
    for cand in (1024, 512, 256, 128):
        if cand <= cap and nc % cand == 0 and t % cand == 0:
            return cand
    raise ValueError("unsupported sequence lengths")


def kernel(x, c, ctx, c_ctx, w_ada, b_ada, w_in, b_in, q_norm, k_norm, gmlp_norm, gmlp_ws, gmlp_b, mix_norm, w_out,
           router_w, router_b, w_gate_up, b_gate_up, w_down, b_down, final_norm):
    b, t, d = x.shape
    tc = ctx.shape[1]
    depth = w_in.shape[0]
    nc, nl = b * tc, b * t
    n = nc + nl
    tm = _tile(nc, t, 512)
    rs = _tile(tc, t, 256)
    tq = _tile(tc, t, 256)
    tk = _tile(t, t, 512)
    tmb = 512
    tt = _tile(nc, t, 256)
    nct, tpb = nc // tm, t // tm
    n_blocks = -(-n * TOP_K // tmb) + N_EXPERTS + 1

    def heads_0213(a, axis):
        blk = [lax.slice_in_dim(a, i * HEAD_DIM, (i + 1) * HEAD_DIM, axis=axis) for i in (0, 2, 1, 3)]
        rest = lax.slice_in_dim(a, MLSTM_HEADS * HEAD_DIM, a.shape[axis], axis=axis)
        return jnp.concatenate(blk + [rest], axis=axis)

    def cols(a):
        pad = jnp.zeros(a.shape[:-1] + (LANES - 16,), a.dtype)
        return jnp.concatenate([a[..., 0:768], heads_0213(a[..., 768:1024], a.ndim - 1), a[..., 1040:2320],
                                a[..., 1024:1040], pad], axis=-1)

    w_in_r = cols(w_in).astype(BF16)
    b_in_r = cols(b_in).reshape(depth, 1, C_END)
    qn = jnp.tile(q_norm, (1, ATTN_HEADS)).reshape(depth, 1, 512)
    kn = jnp.tile(k_norm, (1, 2)).reshape(depth, 1, LANES)
    gn = gmlp_norm.reshape(depth, 1, 256)
    ws4 = gmlp_ws.reshape(depth, GMLP_GROUPS * GMLP_CHUNK, GMLP_CHUNK).astype(BF16)
    gb = jnp.repeat(jnp.swapaxes(gmlp_b, 1, 2), HEAD_DIM, axis=2)
    mn = heads_0213(mix_norm, 1).reshape(depth, 1, d)
    wo = heads_0213(w_out, 1).astype(BF16)
    rw = jnp.pad(router_w, ((0, 0), (0, 0), (0, LANES - N_EXPERTS)))
    rwh = rw.astype(BF16)
    rwl = (rw - rwh.astype(F32)).astype(BF16)
    rb = jnp.pad(router_b, ((0, 0), (0, LANES - N_EXPERTS))).reshape(depth, 1, LANES)
    wgl = _wprep_call(w_gate_up)
    wd = w_down.astype(BF16)
    ne, ff = w_down.shape[1], w_down.shape[2]
    bgl = jnp.concatenate([b_gate_up[..., 0::2], b_gate_up[..., 1::2]], axis=-1).reshape(depth, ne, 1, 2 * ff)
    bd = b_down.reshape(depth, ne, 1, d)
    cos_t, sin_t = _rope_tables(t, tm)

    r = -(-(b + 1) // 8) * 8
    cs = jnp.zeros((r, d), F32).at[0].set(c_ctx).at[1:b + 1].set(c)
    mods = _ada_call(cs, w_ada, b_ada).reshape(depth, r, 6, d)

    xs = jnp.concatenate([ctx.reshape(nc, d), x.reshape(nl, d)], axis=0)
    for l in range(depth):
        qtd, mk, vt2, gt3, mo, g, gm, aq, akd, avd = _proj_call(
            xs, mods, w_in_r, b_in_r, qn, kn, gn, ws4, gb, cos_t, sin_t, l=l, tm=tm, nct=nct, tpb=tpb)
        hf, hb = _mlstm_call(mk, qtd, vt2, g, gt3, b=b, rs=rs, nsc=tc // rs, nsl=t // rs, ncb=nc // rs)
        a = _attn_call(aq, akd, avd, b=b, tq=tq, tc=tc, t=t, tk=tk, ncq=nc // tq)
        xs, h2, logits = _merge_call(hf, hb, mo, gm, a, xs, mods, mn, wo, rwh, rwl, rb, l=l, tm=tm, nct=nct, tpb=tpb)
        rn, rt, cnt = _route_call(logits, tr=tm)
        dest_t, slot_tok, block_expert, nvalid = _slots(rt, cnt, tmb, n_blocks)
        y_slots = _moe_call(block_expert, nvalid, slot_tok.reshape(n_blocks, 1, tmb),
                            h2.reshape(n, d // LANES, LANES), wgl, wd, bgl, bd, l=l, tmb=tmb)
        dest_blk = dest_t.reshape(TOP_K, n // tt, tt).transpose(1, 0, 2).reshape(n // tt, 1, TOP_K * tt)
        xs = _combine_call(dest_blk, y_slots.reshape(n_blocks * tmb, d // LANES, LANES), rn, xs, mods, l=l, tt=tt,
                           nct=nc // tt, tpb=t // tt)
    out = _final_call(xs, final_norm, tm=tm, nct=nct, nl=nl)
    return out.reshape(b, t, d)
```
